```python
import jax
import jax.numpy as jnp
from jax import lax
import numpy as np

D_MODEL = 1024
BATCH = 4
SEQ = 4096
DEPTH = 1

ATT_HEADS = 8
ATT_KV_HEADS = 2
ATT_GROUP = ATT_HEADS // ATT_KV_HEADS
ATT_HEAD_DIM = 64
WINDOW = 128
ATT_BLOCK = 128
ROPE_THETA = 500000.0
ROPE_DIM = ATT_HEAD_DIM // 4

DN_HEADS = 4
DN_KEY_DIM = 128
DN_VAL_DIM = 128
DN_CONV = 4
DN_CHUNK = 64

ATT_Q_W = ATT_HEADS * ATT_HEAD_DIM
ATT_KV_W = ATT_KV_HEADS * ATT_HEAD_DIM
DN_QK_W = DN_HEADS * DN_KEY_DIM
DN_V_W = DN_HEADS * DN_VAL_DIM
DN_QKV_W = 2 * DN_QK_W + DN_V_W
MIX_W = ATT_Q_W + DN_V_W
IN_W = ATT_Q_W + 2 * ATT_KV_W + DN_QKV_W + DN_V_W + 2 * DN_HEADS
IN_SPLITS = (ATT_Q_W,
             ATT_Q_W + ATT_KV_W,
             ATT_Q_W + 2 * ATT_KV_W,
             ATT_Q_W + 2 * ATT_KV_W + DN_QKV_W,
             ATT_Q_W + 2 * ATT_KV_W + DN_QKV_W + DN_V_W,
             ATT_Q_W + 2 * ATT_KV_W + DN_QKV_W + DN_V_W + DN_HEADS)

PEER_HEADS = 8
PEER_KEYS = 128
PEER_EXPERTS = PEER_KEYS * PEER_KEYS
PEER_QDIM = 256
PEER_HALF = PEER_QDIM // 2
PEER_TOPK = 16
PEER_TOKEN_BLOCK = 128

EPS = 1e-6

kernel_name = "hybrid_swa_sink_gdn_peer"


def rms_norm(x, w):
    xf = x.astype(jnp.float32)
    y = xf * lax.rsqrt(jnp.mean(xf * xf, axis=-1, keepdims=True) + EPS)
    return (y * w.astype(jnp.float32)).astype(x.dtype)


def l2_norm(x):
    return x * lax.rsqrt(jnp.sum(x * x, axis=-1, keepdims=True) + EPS)


def partial_rope(x, pos):
    half = ROPE_DIM // 2
    inv_freq = ROPE_THETA ** (-(jnp.arange(half, dtype=jnp.float32) * 2.0 / ROPE_DIM))
    ang = pos.astype(jnp.float32)[:, None] * inv_freq[None, :]
    cos = jnp.cos(ang)[None, :, None, :]
    sin = jnp.sin(ang)[None, :, None, :]
    xr = x[..., :ROPE_DIM].astype(jnp.float32)
    x1, x2 = xr[..., :half], xr[..., half:]
    rot = jnp.concatenate([x1 * cos - x2 * sin, x2 * cos + x1 * sin], axis=-1)
    return jnp.concatenate([rot.astype(x.dtype), x[..., ROPE_DIM:]], axis=-1)


def sliding_window_sink_attention(q, k, v, sinks):
    B, T = q.shape[0], q.shape[1]
    nb = T // ATT_BLOCK
    qb = q.reshape(B, nb, ATT_BLOCK, ATT_KV_HEADS, ATT_GROUP, ATT_HEAD_DIM)

    def band(t):
        tb = t.reshape(B, nb, ATT_BLOCK, ATT_KV_HEADS, ATT_HEAD_DIM)
        prev = jnp.pad(tb, ((0, 0), (1, 0), (0, 0), (0, 0), (0, 0)))[:, :-1]
        return jnp.concatenate([prev, tb], axis=2)

    kb, vb = band(k), band(v)
    s = jnp.einsum('bnqhgd,bnkhd->bnhgqk', qb, kb,
                   preferred_element_type=jnp.float32) * (ATT_HEAD_DIM ** -0.5)
    qi = jnp.arange(ATT_BLOCK)[:, None]
    ki = jnp.arange(2 * ATT_BLOCK)[None, :]
    rel = qi + ATT_BLOCK - ki
    kpos = (jnp.arange(nb)[:, None, None] - 1) * ATT_BLOCK + ki[None]
    valid = (rel >= 0)[None] & (rel < WINDOW)[None] & (kpos >= 0)
    s = jnp.where(valid[None, :, None, None], s, -jnp.inf)
    sink = sinks.astype(jnp.float32).reshape(1, 1, ATT_KV_HEADS, ATT_GROUP, 1, 1)
    m = jnp.maximum(jnp.max(s, axis=-1, keepdims=True), sink)
    p = jnp.exp(s - m)
    probs = p / (jnp.sum(p, axis=-1, keepdims=True) + jnp.exp(sink - m))
    o = jnp.einsum('bnhgqk,bnkhd->bnqhgd', probs.astype(v.dtype), vb)
    return o.reshape(B, T, ATT_HEADS * ATT_HEAD_DIM)


def causal_silu_conv(x, w):
    C = x.shape[-1]
    y = lax.conv_general_dilated(x, w[:, None, :].astype(x.dtype), window_strides=(1,),
                                 padding=((DN_CONV - 1, 0),),
                                 dimension_numbers=('NWC', 'WIO', 'NWC'),
                                 feature_group_count=C)
    return jax.nn.silu(y)


def chunk_gated_delta_rule(q, k, v, g, beta):
    B, T, H, Dk = q.shape
    Dv = v.shape[-1]
    C = DN_CHUNK
    n = T // C
    f32 = jnp.float32
    q = l2_norm(q.astype(f32)) * (Dk ** -0.5)
    k = l2_norm(k.astype(f32))
    v = v.astype(f32)

    def to_chunks(t):
        return jnp.swapaxes(t.reshape((B, n, C) + t.shape[2:]), 2, 3)

    q, k, v, g, beta = to_chunks(q), to_chunks(k), to_chunks(v), to_chunks(g), to_chunks(beta)
    g = jnp.cumsum(g, axis=-1)
    tril = jnp.tril(jnp.ones((C, C), dtype=bool))
    strict = jnp.tril(jnp.ones((C, C), dtype=bool), -1)
    diff = g[..., :, None] - g[..., None, :]
    decay = jnp.where(tril, jnp.exp(jnp.where(tril, diff, 0.0)), 0.0)
    k_beta = k * beta[..., None]
    v_beta = v * beta[..., None]
    L = jnp.where(strict, jnp.einsum('bnhid,bnhjd->bnhij', k_beta, k) * decay, 0.0)
    eye = jnp.eye(C, dtype=f32)
    t_mat = lax.linalg.triangular_solve(eye + L, jnp.broadcast_to(eye, L.shape),
                                        left_side=True, lower=True, unit_diagonal=True)
    u = t_mat @ v_beta
    w = t_mat @ (k_beta * jnp.exp(g)[..., None])
    intra = jnp.where(tril, jnp.einsum('bnhid,bnhjd->bnhij', q, k) * decay, 0.0)
    q_dec = q * jnp.exp(g)[..., None]
    k_dec = k * jnp.exp(g[..., -1:] - g)[..., None]
    g_last = jnp.exp(g[..., -1])

    def step(S, inp):
        intra_c, qd, kd, u_c, w_c, gl = inp
        v_new = u_c - w_c @ S
        o = qd @ S + intra_c @ v_new
        S = S * gl[..., None, None] + jnp.swapaxes(kd, -1, -2) @ v_new
        return S, o

    xs = (jnp.moveaxis(intra, 1, 0), jnp.moveaxis(q_dec, 1, 0), jnp.moveaxis(k_dec, 1, 0),
          jnp.moveaxis(u, 1, 0), jnp.moveaxis(w, 1, 0), jnp.moveaxis(g_last, 1, 0))
    S0 = jnp.zeros((B, H, Dk, Dv), f32)
    _, o = lax.scan(step, S0, xs)
    o = jnp.swapaxes(jnp.moveaxis(o, 0, 1), 2, 3)
    return o.reshape(B, T, H, Dv)


def peer(h, w_query, sub_keys1, sub_keys2, expert_down, expert_up):
    B, T, D = h.shape
    N = B * T
    hf = h.reshape(N, D)
    qry = (hf @ w_query).reshape(N, PEER_HEADS, 2, PEER_HALF)
    s1 = jnp.einsum('nhc,kc->nhk', qry[:, :, 0], sub_keys1, preferred_element_type=jnp.float32)
    s2 = jnp.einsum('nhc,kc->nhk', qry[:, :, 1], sub_keys2, preferred_element_type=jnp.float32)
    v1, i1 = lax.top_k(s1, PEER_TOPK)
    v2, i2 = lax.top_k(s2, PEER_TOPK)
    cand = (v1[..., :, None] + v2[..., None, :]).reshape(N, PEER_HEADS, PEER_TOPK * PEER_TOPK)
    cand_idx = (i1[..., :, None] * PEER_KEYS + i2[..., None, :]).reshape(
        N, PEER_HEADS, PEER_TOPK * PEER_TOPK)
    top_s, sel = lax.top_k(cand, PEER_TOPK)
    idx = jnp.take_along_axis(cand_idx, sel, axis=-1)
    gates = jax.nn.softmax(top_s, axis=-1)
    hk = PEER_HEADS * PEER_TOPK
    nb = N // PEER_TOKEN_BLOCK

    def block(args):
        xb, ib, gb = args
        u = jnp.take(expert_down, ib, axis=0)
        act = jax.nn.gelu(jnp.einsum('td,tkd->tk', xb, u), approximate=False)
        vsel = jnp.take(expert_up, ib, axis=0)
        return jnp.einsum('tk,tkd->td', (gb * act).astype(xb.dtype), vsel)

    out = lax.map(block, (hf.reshape(nb, PEER_TOKEN_BLOCK, D),
                          idx.reshape(nb, PEER_TOKEN_BLOCK, hk),
                          gates.reshape(nb, PEER_TOKEN_BLOCK, hk)))
    return out.reshape(B, T, D)


def setup_inputs(seed: int = 0):
    key = jax.random.key(seed)
    ks = jax.random.split(key, 17)
    f32 = jnp.float32

    def nrm(k, shape, scale):
        return jax.random.normal(k, shape, f32) * scale

    def gain(k, shape):
        return 1.0 + 0.02 * jax.random.normal(k, shape, f32)

    return {
        'x': nrm(ks[0], (BATCH, SEQ, D_MODEL), 1.0),
        'norm1_w': gain(ks[1], (DEPTH, D_MODEL)),
        'w_in': nrm(ks[2], (DEPTH, D_MODEL, IN_W), D_MODEL ** -0.5),
        'att_q_norm': gain(ks[3], (DEPTH, ATT_HEAD_DIM)),
        'att_k_norm': gain(ks[4], (DEPTH, ATT_HEAD_DIM)),
        'att_sinks': nrm(ks[5], (DEPTH, ATT_HEADS), 0.5),
        'dn_conv_w': nrm(ks[6], (DEPTH, DN_CONV, DN_QKV_W), DN_CONV ** -0.5),
        'dn_A_log': jnp.log(jax.random.uniform(ks[7], (DEPTH, DN_HEADS), f32, 1.0, 16.0)),
        'dn_dt_bias': 1.0 + 0.1 * jax.random.normal(ks[8], (DEPTH, DN_HEADS), f32),
        'dn_out_norm': gain(ks[9], (DEPTH, DN_VAL_DIM)),
        'w_out': nrm(ks[10], (DEPTH, MIX_W, D_MODEL), MIX_W ** -0.5),
        'norm2_w': gain(ks[11], (DEPTH, D_MODEL)),
        'peer_w_query': nrm(ks[12], (DEPTH, D_MODEL, PEER_HEADS * PEER_QDIM), D_MODEL ** -0.5),
        'peer_sub_keys1': nrm(ks[13], (DEPTH, PEER_KEYS, PEER_HALF), PEER_HALF ** -0.5),
        'peer_sub_keys2': nrm(ks[14], (DEPTH, PEER_KEYS, PEER_HALF), PEER_HALF ** -0.5),
        'peer_expert_down': nrm(ks[15], (DEPTH, PEER_EXPERTS, D_MODEL), D_MODEL ** -0.5),
        'peer_expert_up': nrm(ks[16], (DEPTH, PEER_EXPERTS, D_MODEL), PEER_HEADS ** -0.5),
    }


def reference(x, norm1_w, w_in, att_q_norm, att_k_norm, att_sinks, dn_conv_w, dn_A_log,
              dn_dt_bias, dn_out_norm, w_out, norm2_w, peer_w_query, peer_sub_keys1,
              peer_sub_keys2, peer_expert_down, peer_expert_up):
    B, T, _ = x.shape
    pos = jnp.arange(T)
    for l in range(DEPTH):
        h = rms_norm(x, norm1_w[l])
        proj = h @ w_in[l]
        a_q, a_k, a_v, d_qkv, d_gate, d_b, d_a = jnp.split(proj, IN_SPLITS, axis=-1)

        q = a_q.reshape(B, T, ATT_HEADS, ATT_HEAD_DIM)
        k = a_k.reshape(B, T, ATT_KV_HEADS, ATT_HEAD_DIM)
        v = a_v.reshape(B, T, ATT_KV_HEADS, ATT_HEAD_DIM)
        q = partial_rope(rms_norm(q, att_q_norm[l]), pos)
        k = partial_rope(rms_norm(k, att_k_norm[l]), pos)
        att_out = sliding_window_sink_attention(q, k, v, att_sinks[l])

        d_qkv = causal_silu_conv(d_qkv, dn_conv_w[l])
        dq, dk, dv = jnp.split(d_qkv, (DN_QK_W, 2 * DN_QK_W), axis=-1)
        beta = jax.nn.sigmoid(d_b.astype(jnp.float32))
        g = -jnp.exp(dn_A_log[l].astype(jnp.float32)) * jax.nn.softplus(
            d_a.astype(jnp.float32) + dn_dt_bias[l].astype(jnp.float32))
        o = chunk_gated_delta_rule(dq.reshape(B, T, DN_HEADS, DN_KEY_DIM),
                                   dk.reshape(B, T, DN_HEADS, DN_KEY_DIM),
                                   dv.reshape(B, T, DN_HEADS, DN_VAL_DIM), g, beta)
        o = rms_norm(o, dn_out_norm[l]) * jax.nn.silu(
            d_gate.reshape(B, T, DN_HEADS, DN_VAL_DIM).astype(jnp.float32))
        dn_out = o.reshape(B, T, DN_V_W).astype(x.dtype)

        x = x + jnp.concatenate([att_out, dn_out], axis=-1) @ w_out[l]

        x = x + peer(rms_norm(x, norm2_w[l]), peer_w_query[l], peer_sub_keys1[l],
                     peer_sub_keys2[l], peer_expert_down[l], peer_expert_up[l])
    return x
```

```python
import functools

import jax
import jax.numpy as jnp
from jax import lax
from jax.experimental import pallas as pl
from jax.experimental.pallas import tpu as pltpu

EPS = 1e-6
LANES = 128
SUBLANES = 8
VMEM_LIMIT = 56 * 1024 * 1024

ATT_HEADS, ATT_KV_HEADS, ATT_HEAD_DIM = 8, 2, 64
ATT_GROUP = ATT_HEADS // ATT_KV_HEADS
ATT_BLOCK = 128
ROPE_THETA = 500000.0
ROPE_DIM = ATT_HEAD_DIM // 4
DN_HEADS, DN_DIM, DN_CONV, DN_CHUNK = 4, 128, 4, 64
ATT_Q_W = ATT_HEADS * ATT_HEAD_DIM
ATT_KV_W = ATT_KV_HEADS * ATT_HEAD_DIM
DN_W = DN_HEADS * DN_DIM
DN_QKV_W = 3 * DN_W
IN_MAIN_W = ATT_Q_W + 2 * ATT_KV_W + DN_QKV_W + DN_W
PEER_HEADS, PEER_KEYS, PEER_TOPK = 8, 128, 16
PEER_HALF = 128

F32 = jnp.float32
BF16 = jnp.bfloat16
HIGHEST = lax.Precision.HIGHEST
NT_DIMS = (((1,), (1,)), ((), ()))
TN_DIMS = (((0,), (0,)), ((), ()))


def _params(*sem):
    return pltpu.CompilerParams(dimension_semantics=sem, vmem_limit_bytes=VMEM_LIMIT)


def _bdot(a, b):
    return jnp.dot(a.astype(BF16), b.astype(BF16), preferred_element_type=F32)


def _fdot(a, b):
    return jnp.dot(a, b, precision=HIGHEST, preferred_element_type=F32)


def _in_proj_kernel(x_ref, nw_ref, w_ref, q_ref, k_ref, v_ref, dqkv_ref, gate_ref, ba_ref):
    x = x_ref[...]
    h = x * lax.rsqrt(jnp.mean(x * x, axis=-1, keepdims=True) + EPS) * nw_ref[...]
    p = jnp.dot(h.astype(BF16), w_ref[...], preferred_element_type=F32)
    o = 0
    for ref in (q_ref, k_ref, v_ref, dqkv_ref, gate_ref, ba_ref):
        w = ref.shape[-1]
        ref[...] = p[:, o:o + w]
        o += w


def in_proj(x2, norm_w, w_in, tm=512):
    n, d = x2.shape
    small = w_in.shape[1] - IN_MAIN_W
    w_pad = jnp.pad(w_in, ((0, 0), (0, LANES - small))).astype(BF16)
    widths = (ATT_Q_W, ATT_KV_W, ATT_KV_W, DN_QKV_W, DN_W, LANES)
    return pl.pallas_call(
        _in_proj_kernel,
        grid=(n // tm,),
        in_specs=[pl.BlockSpec((tm, d), lambda i: (i, 0)),
                  pl.BlockSpec((1, d), lambda i: (0, 0)),
                  pl.BlockSpec(w_pad.shape, lambda i: (0, 0))],
        out_specs=[pl.BlockSpec((tm, w), lambda i: (i, 0)) for w in widths],
        out_shape=[jax.ShapeDtypeStruct((n, w), F32) for w in widths],
        compiler_params=_params("parallel"),
        name="in_proj",
    )(x2, norm_w.reshape(1, d), w_pad)


def _norm_rope(x, w, c, sm, sp):
    rows, width = x.shape
    x2 = x * x
    parts = []
    for h in range(width // ATT_HEAD_DIM):
        ms = jnp.mean(x2[:, h * ATT_HEAD_DIM:(h + 1) * ATT_HEAD_DIM], axis=-1, keepdims=True)
        parts.append(jnp.broadcast_to(lax.rsqrt(ms + EPS), (rows, ATT_HEAD_DIM)))
    xn = x * jnp.concatenate(parts, axis=1) * w
    half = ROPE_DIM // 2
    return xn * c + pltpu.roll(xn, width - half, 1) * sm + pltpu.roll(xn, half, 1) * sp


def _attn_kernel(q_ref, kc_ref, kp_ref, vc_ref, vp_ref, c_ref, sm_ref, sp_ref,
                 cp_ref, smp_ref, spp_ref, qw_ref, kw_ref, sink_ref, o_ref):
    i = pl.program_id(1)
    blk = ATT_BLOCK
    q = _norm_rope(q_ref[0], qw_ref[...], c_ref[...], sm_ref[...], sp_ref[...])
    kw = kw_ref[...]
    kc = _norm_rope(kc_ref[0], kw, c_ref[:, :ATT_KV_W], sm_ref[:, :ATT_KV_W], sp_ref[:, :ATT_KV_W])
    kp = _norm_rope(kp_ref[0], kw, cp_ref[...], smp_ref[...], spp_ref[...])
    vc = vc_ref[0]
    vp = vp_ref[0]
    rows = ATT_GROUP * blk
    qi = lax.broadcasted_iota(jnp.int32, (rows, blk), 0) % blk
    kj = lax.broadcasted_iota(jnp.int32, (rows, blk), 1)
    mask_c = kj <= qi
    mask_p = jnp.logical_and(kj > qi, i > 0)
    scale = ATT_HEAD_DIM ** -0.5
    outs = []
    for g in range(ATT_KV_HEADS):
        heads = range(g * ATT_GROUP, (g + 1) * ATT_GROUP)
        qg = jnp.concatenate([q[:, h * ATT_HEAD_DIM:(h + 1) * ATT_HEAD_DIM] for h in heads], axis=0)
        sink = jnp.concatenate([jnp.broadcast_to(sink_ref[h:h + 1, 0:1], (blk, 1)) for h in heads], axis=0)
        lo, hi = g * ATT_HEAD_DIM, (g + 1) * ATT_HEAD_DIM
        qb = qg.astype(BF16)
        s_c = lax.dot_general(qb, kc[:, lo:hi].astype(BF16), NT_DIMS, preferred_element_type=F32) * scale
        s_p = lax.dot_general(qb, kp[:, lo:hi].astype(BF16), NT_DIMS, preferred_element_type=F32) * scale
        s_c = jnp.where(mask_c, s_c, -jnp.inf)
        s_p = jnp.where(mask_p, s_p, -jnp.inf)
        m = jnp.maximum(jnp.maximum(jnp.max(s_c, axis=-1, keepdims=True),
                                    jnp.max(s_p, axis=-1, keepdims=True)), sink)
        p_c = jnp.exp(s_c - m)
        p_p = jnp.exp(s_p - m)
        denom = (jnp.sum(p_c, axis=-1, keepdims=True) + jnp.sum(p_p, axis=-1, keepdims=True)
                 + jnp.exp(sink - m))
        og = (_bdot(p_c, vc[:, lo:hi]) + _bdot(p_p, vp[:, lo:hi])) / denom
        outs += [og[j * blk:(j + 1) * blk] for j in range(ATT_GROUP)]
    o_ref[0] = jnp.concatenate(outs, axis=1).astype(o_ref.dtype)


def _rope_tables(t):
    half = ROPE_DIM // 2
    inv_freq = ROPE_THETA ** (-(jnp.arange(half, dtype=F32) * 2.0 / ROPE_DIM))
    ang = jnp.arange(t).astype(F32)[:, None] * inv_freq[None, :]
    cos, sin = jnp.cos(ang), jnp.sin(ang)
    rest = ATT_HEAD_DIM - ROPE_DIM
    c = jnp.concatenate([cos, cos, jnp.ones((t, rest), F32)], axis=1)
    sm = jnp.concatenate([-sin, jnp.zeros((t, half + rest), F32)], axis=1)
    sp = jnp.concatenate([jnp.zeros((t, half), F32), sin, jnp.zeros((t, rest), F32)], axis=1)
    return tuple(jnp.tile(a, (1, ATT_HEADS)) for a in (c, sm, sp))


def attention(q, k, v, q_norm, k_norm, sinks):
    b, t, _ = q.shape
    blk = ATT_BLOCK
    c, sm, sp = _rope_tables(t)
    cur = lambda bi, i: (bi, i, 0)
    prev = lambda bi, i: (bi, jnp.maximum(i - 1, 0), 0)
    tcur = lambda bi, i: (i, 0)
    tprev = lambda bi, i: (jnp.maximum(i - 1, 0), 0)
    const = lambda bi, i: (0, 0)
    return pl.pallas_call(
        _attn_kernel,
        grid=(b, t // blk),
        in_specs=[pl.BlockSpec((1, blk, ATT_Q_W), cur),
                  pl.BlockSpec((1, blk, ATT_KV_W), cur), pl.BlockSpec((1, blk, ATT_KV_W), prev),
                  pl.BlockSpec((1, blk, ATT_KV_W), cur), pl.BlockSpec((1, blk, ATT_KV_W), prev),
                  pl.BlockSpec((blk, ATT_Q_W), tcur), pl.BlockSpec((blk, ATT_Q_W), tcur),
                  pl.BlockSpec((blk, ATT_Q_W), tcur),
                  pl.BlockSpec((blk, ATT_KV_W), tprev), pl.BlockSpec((blk, ATT_KV_W), tprev),
                  pl.BlockSpec((blk, ATT_KV_W), tprev),
                  pl.BlockSpec((1, ATT_Q_W), const), pl.BlockSpec((1, ATT_KV_W), const),
                  pl.BlockSpec((ATT_HEADS, LANES), const)],
        out_specs=pl.BlockSpec((1, blk, ATT_Q_W), cur),
        out_shape=jax.ShapeDtypeStruct((b, t, ATT_Q_W), BF16),
        compiler_params=_params("parallel", "parallel"),
        name="swa_attention",
    )(q, k, k, v, v, c, sm, sp, c, sm, sp,
      jnp.tile(q_norm, ATT_HEADS).reshape(1, ATT_Q_W), jnp.tile(k_norm, ATT_KV_HEADS).reshape(1, ATT_KV_W),
      jnp.broadcast_to(sinks[:, None], (ATT_HEADS, LANES)))


DN_CHUNKS_PER_STEP = 2


def _softplus(x):
    return jnp.maximum(x, 0.0) + jnp.log1p(jnp.exp(-jnp.abs(x)))


def _dn_kernel(x_ref, gate_ref, ba_ref, cw_ref, alog_ref, dtb_ref, onw_ref, o_ref, halo_ref, s_ref):
    c = pl.program_id(1)
    ck = DN_CHUNK
    rows = x_ref.shape[1]

    @pl.when(c == 0)
    def _():
        halo_ref[...] = jnp.zeros_like(halo_ref)
        s_ref[...] = jnp.zeros_like(s_ref)

    x = x_ref[0]
    xe = jnp.concatenate([halo_ref[...], x], axis=0)
    halo_ref[...] = x[rows - SUBLANES:rows]
    cw = cw_ref[...]
    y = x * cw[DN_CONV - 1:DN_CONV]
    for s in range(1, DN_CONV):
        y = y + xe[SUBLANES - s:SUBLANES - s + rows] * cw[DN_CONV - 1 - s:DN_CONV - s]
    y = y * jax.nn.sigmoid(y)

    ba = ba_ref[0]
    beta_all = jax.nn.sigmoid(ba)
    g_all = -jnp.exp(alog_ref[...]) * _softplus(ba + dtb_ref[...])
    gate = gate_ref[0]
    onw = onw_ref[...]

    ri = lax.broadcasted_iota(jnp.int32, (ck, ck), 0)
    ci = lax.broadcasted_iota(jnp.int32, (ck, ck), 1)
    tril = ri >= ci
    triu = ri <= ci
    strict = ri > ci
    tril_f = jnp.where(tril, 1.0, 0.0).astype(F32)
    eye_f = jnp.where(ri == ci, 1.0, 0.0).astype(F32)
    ones_f = jnp.ones((ck, ck), F32)

    for cc in range(rows // ck):
        r0 = cc * ck
        for h in range(DN_HEADS):
            q = y[r0:r0 + ck, h * DN_DIM:(h + 1) * DN_DIM]
            k = y[r0:r0 + ck, DN_W + h * DN_DIM:DN_W + (h + 1) * DN_DIM]
            v = y[r0:r0 + ck, 2 * DN_W + h * DN_DIM:2 * DN_W + (h + 1) * DN_DIM]
            q = q * lax.rsqrt(jnp.sum(q * q, axis=-1, keepdims=True) + EPS) * (DN_DIM ** -0.5)
            k = k * lax.rsqrt(jnp.sum(k * k, axis=-1, keepdims=True) + EPS)
            beta = beta_all[r0:r0 + ck, h:h + 1]
            g = g_all[r0:r0 + ck, DN_HEADS + h:DN_HEADS + h + 1]
            gb = jnp.broadcast_to(g, (ck, ck))
            gc_i = _fdot(tril_f, gb)
            gc_j = _fdot(ones_f, jnp.where(triu, gb, 0.0))
            decay = jnp.where(tril, jnp.exp(jnp.where(tril, gc_i - gc_j, 0.0)), 0.0)
            gcol = gc_i[:, 0:1]
            glast = gc_j[:, ck - 1:ck]
            kb = k * beta
            vb = v * beta
            a = -jnp.where(strict, lax.dot_general(kb.astype(BF16), k.astype(BF16), NT_DIMS,
                                                   preferred_element_type=F32) * decay, 0.0)
            t_mat = eye_f + a
            pw = a
            for _ in range(5):
                pw = _fdot(pw, pw)
                t_mat = t_mat + _fdot(t_mat, pw)
            u = _bdot(t_mat, vb)
            w = _bdot(t_mat, kb * jnp.exp(gcol))
            intra = jnp.where(tril, lax.dot_general(q.astype(BF16), k.astype(BF16), NT_DIMS,
                                                    preferred_element_type=F32) * decay, 0.0)
            qd = q * jnp.exp(gcol)
            kd = k * jnp.exp(glast - gcol)
            st = s_ref[h]
            v_new = u - _bdot(w, st)
            o = _bdot(qd, st) + _bdot(intra, v_new)
            egl = jnp.exp(jnp.concatenate([glast, glast], axis=0))
            s_ref[h] = st * egl + lax.dot_general(kd.astype(BF16), v_new.astype(BF16), TN_DIMS,
                                                  preferred_element_type=F32)
            on = o * lax.rsqrt(jnp.mean(o * o, axis=-1, keepdims=True) + EPS) * onw
            gt = gate[r0:r0 + ck, h * DN_DIM:(h + 1) * DN_DIM]
            o_ref[0, r0:r0 + ck, h * DN_DIM:(h + 1) * DN_DIM] = (on * (gt * jax.nn.sigmoid(gt))).astype(o_ref.dtype)


def _lane_row(vals, offset):
    return jnp.zeros((1, LANES), F32).at[0, offset:offset + vals.shape[0]].set(vals)


def delta_net(dqkv, gate, ba, conv_w, a_log, dt_bias, out_norm):
    b, t, _ = dqkv.shape
    rows = DN_CHUNK * DN_CHUNKS_PER_STEP
    blk = lambda bi, c: (bi, c, 0)
    const = lambda bi, c: (0, 0)
    return pl.pallas_call(
        _dn_kernel,
        grid=(b, t // rows),
        in_specs=[pl.BlockSpec((1, rows, DN_QKV_W), blk), pl.BlockSpec((1, rows, DN_W), blk),
                  pl.BlockSpec((1, rows, LANES), blk),
                  pl.BlockSpec((DN_CONV, DN_QKV_W), const),
                  pl.BlockSpec((1, LANES), const), pl.BlockSpec((1, LANES), const),
                  pl.BlockSpec((1, DN_DIM), const)],
        out_specs=pl.BlockSpec((1, rows, DN_W), blk),
        out_shape=jax.ShapeDtypeStruct((b, t, DN_W), BF16),
        scratch_shapes=[pltpu.VMEM((SUBLANES, DN_QKV_W), F32), pltpu.VMEM((DN_HEADS, DN_DIM, DN_DIM), F32)],
        compiler_params=_params("parallel", "arbitrary"),
        name="gated_delta_net",
    )(dqkv, gate, ba, conv_w, _lane_row(a_log, DN_HEADS), _lane_row(dt_bias, DN_HEADS),
      out_norm.reshape(1, DN_DIM))


def _mix_kernel(att_ref, dn_ref, x_ref, wa_ref, wd_ref, n2_ref, wq_ref, x1_ref, h2_ref, qry_ref):
    x1 = (x_ref[...] + jnp.dot(att_ref[...], wa_ref[...], preferred_element_type=F32)
          + jnp.dot(dn_ref[...], wd_ref[...], preferred_element_type=F32))
    x1_ref[...] = x1
    h2 = (x1 * lax.rsqrt(jnp.mean(x1 * x1, axis=-1, keepdims=True) + EPS) * n2_ref[...]).astype(BF16)
    h2_ref[...] = h2
    qry_ref[...] = jnp.dot(h2, wq_ref[...], preferred_element_type=F32)


def mix_proj(att, dn, x2, w_out, norm2_w, w_query, tm=512):
    n, d = x2.shape
    wa = w_out[:ATT_Q_W].astype(BF16)
    wd = w_out[ATT_Q_W:].astype(BF16)
    wq = w_query.astype(BF16)
    qw = wq.shape[1]
    row = lambda i: (i, 0)
    const = lambda i: (0, 0)
    return pl.pallas_call(
        _mix_kernel,
        grid=(n // tm,),
        in_specs=[pl.BlockSpec((tm, ATT_Q_W), row), pl.BlockSpec((tm, DN_W), row), pl.BlockSpec((tm, d), row),
                  pl.BlockSpec(wa.shape, const), pl.BlockSpec(wd.shape, const),
                  pl.BlockSpec((1, d), const), pl.BlockSpec(wq.shape, const)],
        out_specs=[pl.BlockSpec((tm, d), row), pl.BlockSpec((tm, d), row), pl.BlockSpec((tm, qw), row)],
        out_shape=[jax.ShapeDtypeStruct((n, d), F32), jax.ShapeDtypeStruct((n, d), BF16),
                   jax.ShapeDtypeStruct((n, qw), F32)],
        compiler_params=_params("parallel"),
        name="mix_proj",
    )(att, dn, x2, wa, wd, norm2_w.reshape(1, d), wq)


def _top_distinct(s, k):
    vals, cnts = [], []
    x = s
    for _ in range(k):
        m = jnp.max(x, axis=0, keepdims=True)
        eq = x == m
        cnts.append(jnp.sum(jnp.where(eq, 1.0, 0.0), axis=0, keepdims=True))
        vals.append(m)
        x = jnp.where(eq, -jnp.inf, x)
    return jnp.concatenate(vals, axis=0), jnp.concatenate(cnts, axis=0)


def _route_kernel(q_ref, k1_ref, k2_ref, th_ref, e1_ref, s2_ref, e2_ref):
    q = q_ref[...].astype(BF16)
    s1 = lax.dot_general(k1_ref[...], q[:, :PEER_HALF], NT_DIMS, preferred_element_type=F32)
    s2 = lax.dot_general(k2_ref[...], q[:, PEER_HALF:], NT_DIMS, preferred_element_type=F32)
    k = PEER_TOPK
    v1, c1 = _top_distinct(s1, k)
    v2, c2 = _top_distinct(s2, k)
    cand = jnp.concatenate([v1[r:r + 1] + v2 for r in range(k)], axis=0)
    wt = jnp.concatenate([c1[r:r + 1] * c2 for r in range(k)], axis=0)
    x = cand
    tau = jnp.full_like(v1[0:1], -jnp.inf)
    taken = jnp.zeros_like(tau)
    for _ in range(k):
        m = jnp.max(x, axis=0, keepdims=True)
        eq = x == m
        tau = jnp.where(taken < k, m, tau)
        taken = taken + jnp.sum(jnp.where(eq, wt, 0.0), axis=0, keepdims=True)
        x = jnp.where(eq, -jnp.inf, x)
    top = v1[0:1] + v2[0:1]
    z = jnp.sum(jnp.where(cand >= tau, wt * jnp.exp(cand - top), 0.0), axis=0, keepdims=True)
    theta = jnp.full_like(s1, jnp.inf)
    for r in range(k):
        vr = v2[r:r + 1]
        theta = jnp.minimum(theta, jnp.where(s1 + vr >= tau, vr, jnp.inf))
    th_ref[0] = theta
    e1_ref[0] = jnp.exp(s1 - v1[0:1]) / z
    s2_ref[0] = s2
    e2_ref[0] = jnp.exp(s2 - v2[0:1])


def peer_route(qry, keys1, keys2, tn=256):
    n = qry.shape[0]
    qblk = 2 * PEER_HALF
    out = pl.BlockSpec((1, PEER_KEYS, tn), lambda i, h: (h, 0, i))
    const = lambda i, h: (0, 0)
    return pl.pallas_call(
        _route_kernel,
        grid=(n // tn, PEER_HEADS),
        in_specs=[pl.BlockSpec((tn, qblk), lambda i, h: (i, h)),
                  pl.BlockSpec((PEER_KEYS, PEER_HALF), const), pl.BlockSpec((PEER_KEYS, PEER_HALF), const)],
        out_specs=[out] * 4,
        out_shape=[jax.ShapeDtypeStruct((PEER_HEADS, PEER_KEYS, n), F32)] * 4,
        compiler_params=_params("parallel", "parallel"),
        name="peer_route",
    )(qry, keys1.astype(BF16), keys2.astype(BF16))


PEER_EXPERT_BLOCK = 1024
PEER_TOKEN_TILE = 512


def _expert_kernel(h_ref, x1_ref, wd_ref, wu_ref, th_ref, e1_ref, s2_ref, e2_ref, o_ref,
                   act_ref, hw_ref, acc_ref):
    e = pl.program_id(1)
    tn = h_ref.shape[0]

    @pl.when(e == 0)
    def _():
        acc_ref[...] = jnp.zeros_like(acc_ref)

    act_ref[...] = lax.dot_general(wd_ref[...], h_ref[...], NT_DIMS, preferred_element_type=F32)
    for al in range(PEER_EXPERT_BLOCK // PEER_KEYS):
        r0 = al * PEER_KEYS
        for lt in range(tn // LANES):
            ls = slice(lt * LANES, (lt + 1) * LANES)
            gsum = jnp.zeros((PEER_KEYS, LANES), F32)
            for hd in range(PEER_HEADS):
                sel = s2_ref[hd, :, ls] >= th_ref[hd, al:al + 1, ls]
                gsum = gsum + jnp.where(sel, e2_ref[hd, :, ls] * e1_ref[hd, al:al + 1, ls], 0.0)
            z = act_ref[r0:r0 + PEER_KEYS, ls]
            gelu = 0.5 * z * (1.0 + lax.erf(z * (2.0 ** -0.5)))
            hw_ref[r0:r0 + PEER_KEYS, ls] = (gsum * gelu).astype(BF16)
    acc_ref[...] += jnp.dot(wu_ref[...], hw_ref[...], preferred_element_type=F32)

    @pl.when(e == pl.num_programs(1) - 1)
    def _():
        o_ref[...] = x1_ref[...] + acc_ref[...].T


def peer_experts(h2, x1, expert_down, expert_up, theta, e1, s2, e2):
    n, d = h2.shape
    ne = expert_down.shape[0]
    tn, eb = PEER_TOKEN_TILE, PEER_EXPERT_BLOCK
    ab = eb // PEER_KEYS
    wd = expert_down.astype(BF16)
    wu_t = expert_up.astype(BF16).T
    tok = lambda i, e: (i, 0)
    full = pl.BlockSpec((PEER_HEADS, PEER_KEYS, tn), lambda i, e: (0, 0, i))
    part = pl.BlockSpec((PEER_HEADS, ab, tn), lambda i, e: (0, e, i))
    return pl.pallas_call(
        _expert_kernel,
        grid=(n // tn, ne // eb),
        in_specs=[pl.BlockSpec((tn, d), tok), pl.BlockSpec((tn, d), tok),
                  pl.BlockSpec((eb, d), lambda i, e: (e, 0)), pl.BlockSpec((d, eb), lambda i, e: (0, e)),
                  part, part, full, full],
        out_specs=pl.BlockSpec((tn, d), tok),
        out_shape=jax.ShapeDtypeStruct((n, d), F32),
        scratch_shapes=[pltpu.VMEM((eb, tn), F32), pltpu.VMEM((eb, tn), BF16), pltpu.VMEM((d, tn), F32)],
        compiler_params=_params("parallel", "arbitrary"),
        name="peer_experts",
    )(h2, x1, wd, wu_t, theta, e1, s2, e2)


def kernel(x, norm1_w, w_in, att_q_norm, att_k_norm, att_sinks, dn_conv_w, dn_A_log, dn_dt_bias, dn_out_norm, w_out, norm2_w, peer_w_query, peer_sub_keys1, peer_sub_keys2, peer_expert_down, peer_expert_up):
    b, t, d = x.shape
    x2 = x.reshape(b * t, d)
    for l in range(norm1_w.shape[0]):
        a_q, a_k, a_v, d_qkv, d_gate, d_ba = in_proj(x2, norm1_w[l], w_in[l])
        seq = lambda a: a.reshape(b, t, a.shape[-1])
        att = attention(seq(a_q), seq(a_k), seq(a_v), att_q_norm[l], att_k_norm[l], att_sinks[l])
        dn = delta_net(seq(d_qkv), seq(d_gate), seq(d_ba), dn_conv_w[l], dn_A_log[l], dn_dt_bias[l],
                       dn_out_norm[l])
        x1, h2, qry = mix_proj(att.reshape(b * t, -1), dn.reshape(b * t, -1), x2, w_out[l], norm2_w[l],
                               peer_w_query[l])
        theta, e1, s2, e2 = peer_route(qry, peer_sub_keys1[l], peer_sub_keys2[l])
        x2 = peer_experts(h2, x1, peer_expert_down[l], peer_expert_up[l], theta, e1, s2, e2)
    return x2.reshape(b, t, d)
```

```python
import jax
import jax.numpy as jnp
from jax import lax
from jax.experimental import pallas as pl
from jax.experimental.pallas import tpu as pltpu

EPS = 1e-6
LANES = 128
SUBLANES = 8
VMEM_LIMIT = 56 * 1024 * 1024

ATT_HEADS, ATT_KV_HEADS, ATT_HEAD_DIM = 8, 2, 64
ATT_GROUP = ATT_HEADS // ATT_KV_HEADS
ATT_BLOCK = 128
ROPE_THETA = 500000.0
ROPE_DIM = ATT_HEAD_DIM // 4
DN_HEADS, DN_DIM, DN_CONV, DN_CHUNK = 4, 128, 4, 64
ATT_Q_W = ATT_HEADS * ATT_HEAD_DIM
ATT_KV_W = ATT_KV_HEADS * ATT_HEAD_DIM
DN_W = DN_HEADS * DN_DIM
DN_QKV_W = 3 * DN_W
IN_MAIN_W = ATT_Q_W + 2 * ATT_KV_W + DN_QKV_W + DN_W
PEER_HEADS, PEER_KEYS, PEER_TOPK = 8, 128, 16
PEER_HALF = 128

F32 = jnp.float32
BF16 = jnp.bfloat16
NT_DIMS = (((1,), (1,)), ((), ()))
TN_DIMS = (((0,), (0,)), ((), ()))


def _params(*sem):
    return pltpu.CompilerParams(dimension_semantics=sem, vmem_limit_bytes=VMEM_LIMIT)


def _bdot(a, b):
    return jnp.dot(a.astype(BF16), b.astype(BF16), preferred_element_type=F32)


def _hi_lo(x):
    hi = x.astype(BF16)
    return hi, (x - hi.astype(F32)).astype(BF16)


def _dot_hi(a, b):
    ah, al = _hi_lo(a)
    bh, bl = _hi_lo(b)
    return (jnp.dot(ah, bh, preferred_element_type=F32) + jnp.dot(ah, bl, preferred_element_type=F32)
            + jnp.dot(al, bh, preferred_element_type=F32))


def _dot_exact_lhs(lhs, x):
    hi = x.astype(BF16)
    r = x - hi.astype(F32)
    mid = r.astype(BF16)
    lo = (r - mid.astype(F32)).astype(BF16)
    return (jnp.dot(lhs, hi, preferred_element_type=F32) + jnp.dot(lhs, mid, preferred_element_type=F32)
            + jnp.dot(lhs, lo, preferred_element_type=F32))


def _in_proj_kernel(x_ref, nw_ref, w_ref, q_ref, k_ref, v_ref, dqkv_ref, gate_ref, ba_ref):
    x = x_ref[...]
    h = x * lax.rsqrt(jnp.mean(x * x, axis=-1, keepdims=True) + EPS) * nw_ref[...]
    p = jnp.dot(h.astype(BF16), w_ref[...], preferred_element_type=F32)
    o = 0
    for ref in (q_ref, k_ref, v_ref, dqkv_ref, gate_ref, ba_ref):
        w = ref.shape[-1]
        ref[...] = p[:, o:o + w]
        o += w


def in_proj(x2, norm_w, w_in, tm=512):
    n, d = x2.shape
    small = w_in.shape[1] - IN_MAIN_W
    w_pad = jnp.pad(w_in, ((0, 0), (0, LANES - small))).astype(BF16)
    widths = (ATT_Q_W, ATT_KV_W, ATT_KV_W, DN_QKV_W, DN_W, LANES)
    return pl.pallas_call(
        _in_proj_kernel,
        grid=(n // tm,),
        in_specs=[pl.BlockSpec((tm, d), lambda i: (i, 0)),
                  pl.BlockSpec((1, d), lambda i: (0, 0)),
                  pl.BlockSpec(w_pad.shape, lambda i: (0, 0))],
        out_specs=[pl.BlockSpec((tm, w), lambda i: (i, 0)) for w in widths],
        out_shape=[jax.ShapeDtypeStruct((n, w), F32) for w in widths],
        compiler_params=_params("parallel"),
        name="in_proj",
    )(x2, norm_w.reshape(1, d), w_pad)


def _norm_rope(x, w, c, sm, sp):
    rows, width = x.shape
    x2 = x * x
    parts = []
    for h in range(width // ATT_HEAD_DIM):
        ms = jnp.mean(x2[:, h * ATT_HEAD_DIM:(h + 1) * ATT_HEAD_DIM], axis=-1, keepdims=True)
        parts.append(jnp.broadcast_to(lax.rsqrt(ms + EPS), (rows, ATT_HEAD_DIM)))
    xn = x * jnp.concatenate(parts, axis=1) * w
    half = ROPE_DIM // 2
    return xn * c + pltpu.roll(xn, width - half, 1) * sm + pltpu.roll(xn, half, 1) * sp


def _attn_kernel(q_ref, kc_ref, kp_ref, vc_ref, vp_ref, c_ref, sm_ref, sp_ref,
                 cp_ref, smp_ref, spp_ref, qw_ref, kw_ref, sink_ref, o_ref):
    i = pl.program_id(1)
    blk = ATT_BLOCK
    q = _norm_rope(q_ref[0], qw_ref[...], c_ref[...], sm_ref[...], sp_ref[...])
    kw = kw_ref[...]
    kc = _norm_rope(kc_ref[0], kw, c_ref[:, :ATT_KV_W], sm_ref[:, :ATT_KV_W], sp_ref[:, :ATT_KV_W])
    kp = _norm_rope(kp_ref[0], kw, cp_ref[...], smp_ref[...], spp_ref[...])
    vc = vc_ref[0]
    vp = vp_ref[0]
    rows = ATT_GROUP * blk
    qi = lax.broadcasted_iota(jnp.int32, (rows, blk), 0) % blk
    kj = lax.broadcasted_iota(jnp.int32, (rows, blk), 1)
    mask_c = kj <= qi
    mask_p = jnp.logical_and(kj > qi, i > 0)
    scale = ATT_HEAD_DIM ** -0.5
    outs = []
    for g in range(ATT_KV_HEADS):
        heads = range(g * ATT_GROUP, (g + 1) * ATT_GROUP)
        qg = jnp.concatenate([q[:, h * ATT_HEAD_DIM:(h + 1) * ATT_HEAD_DIM] for h in heads], axis=0)
        sink = jnp.concatenate([jnp.broadcast_to(sink_ref[h:h + 1, 0:1], (blk, 1)) for h in heads], axis=0)
        lo, hi = g * ATT_HEAD_DIM, (g + 1) * ATT_HEAD_DIM
        qb = qg.astype(BF16)
        s_c = lax.dot_general(qb, kc[:, lo:hi].astype(BF16), NT_DIMS, preferred_element_type=F32) * scale
        s_p = lax.dot_general(qb, kp[:, lo:hi].astype(BF16), NT_DIMS, preferred_element_type=F32) * scale
        s_c = jnp.where(mask_c, s_c, -jnp.inf)
        s_p = jnp.where(mask_p, s_p, -jnp.inf)
        m = jnp.maximum(jnp.maximum(jnp.max(s_c, axis=-1, keepdims=True),
                                    jnp.max(s_p, axis=-1, keepdims=True)), sink)
        p_c = jnp.exp(s_c - m)
        p_p = jnp.exp(s_p - m)
        denom = (jnp.sum(p_c, axis=-1, keepdims=True) + jnp.sum(p_p, axis=-1, keepdims=True)
                 + jnp.exp(sink - m))
        og = (_bdot(p_c, vc[:, lo:hi]) + _bdot(p_p, vp[:, lo:hi])) / denom
        outs += [og[j * blk:(j + 1) * blk] for j in range(ATT_GROUP)]
    o_ref[0] = jnp.concatenate(outs, axis=1).astype(o_ref.dtype)


def _rope_tables(t):
    half = ROPE_DIM // 2
    inv_freq = ROPE_THETA ** (-(jnp.arange(half, dtype=F32) * 2.0 / ROPE_DIM))
    ang = jnp.arange(t).astype(F32)[:, None] * inv_freq[None, :]
    cos, sin = jnp.cos(ang), jnp.sin(ang)
    rest = ATT_HEAD_DIM - ROPE_DIM
    c = jnp.concatenate([cos, cos, jnp.ones((t, rest), F32)], axis=1)
    sm = jnp.concatenate([-sin, jnp.zeros((t, half + rest), F32)], axis=1)
    sp = jnp.concatenate([jnp.zeros((t, half), F32), sin, jnp.zeros((t, rest), F32)], axis=1)
    return tuple(jnp.tile(a, (1, ATT_HEADS)) for a in (c, sm, sp))


def attention(q, k, v, q_norm, k_norm, sinks):
    b, t, _ = q.shape
    blk = ATT_BLOCK
    c, sm, sp = _rope_tables(t)
    cur = lambda bi, i: (bi, i, 0)
    prev = lambda bi, i: (bi, jnp.maximum(i - 1, 0), 0)
    tcur = lambda bi, i: (i, 0)
    tprev = lambda bi, i: (jnp.maximum(i - 1, 0), 0)
    const = lambda bi, i: (0, 0)
    return pl.pallas_call(
        _attn_kernel,
        grid=(b, t // blk),
        in_specs=[pl.BlockSpec((1, blk, ATT_Q_W), cur),
                  pl.BlockSpec((1, blk, ATT_KV_W), cur), pl.BlockSpec((1, blk, ATT_KV_W), prev),
                  pl.BlockSpec((1, blk, ATT_KV_W), cur), pl.BlockSpec((1, blk, ATT_KV_W), prev),
                  pl.BlockSpec((blk, ATT_Q_W), tcur), pl.BlockSpec((blk, ATT_Q_W), tcur),
                  pl.BlockSpec((blk, ATT_Q_W), tcur),
                  pl.BlockSpec((blk, ATT_KV_W), tprev), pl.BlockSpec((blk, ATT_KV_W), tprev),
                  pl.BlockSpec((blk, ATT_KV_W), tprev),
                  pl.BlockSpec((1, ATT_Q_W), const), pl.BlockSpec((1, ATT_KV_W), const),
                  pl.BlockSpec((ATT_HEADS, LANES), const)],
        out_specs=pl.BlockSpec((1, blk, ATT_Q_W), cur),
        out_shape=jax.ShapeDtypeStruct((b, t, ATT_Q_W), BF16),
        compiler_params=_params("parallel", "parallel"),
        name="swa_attention",
    )(q, k, k, v, v, c, sm, sp, c, sm, sp,
      jnp.tile(q_norm, ATT_HEADS).reshape(1, ATT_Q_W), jnp.tile(k_norm, ATT_KV_HEADS).reshape(1, ATT_KV_W),
      jnp.broadcast_to(sinks[:, None], (ATT_HEADS, LANES)))


def _softplus(x):
    return jnp.maximum(x, 0.0) + jnp.log1p(jnp.exp(-jnp.abs(x)))


def _dn_kernel(x_ref, gate_ref, ba_ref, cw_ref, alog_ref, dtb_ref, onw_ref, o_ref, *scratch):
    ck = DN_CHUNK
    nb = x_ref.shape[0]
    halo_refs = scratch[:nb]
    s_refs = scratch[nb:]

    @pl.when(pl.program_id(0) == 0)
    def _():
        for ref in scratch:
            ref[...] = jnp.zeros_like(ref)

    cw = cw_ref[...]
    onw = onw_ref[...]
    neg_a = -jnp.exp(alog_ref[...])
    dtb = dtb_ref[...]
    ri = lax.broadcasted_iota(jnp.int32, (ck, ck), 0)
    ci = lax.broadcasted_iota(jnp.int32, (ck, ck), 1)
    tril = ri >= ci
    strict = ri > ci
    tril_b = jnp.where(tril, 1.0, 0.0).astype(BF16)
    eye_f = jnp.where(ri == ci, 1.0, 0.0).astype(F32)

    chains = [(b, h) for b in range(nb) for h in range(DN_HEADS)]
    ys, betas, gs_all = [], [], []
    for b in range(nb):
        x = x_ref[b]
        xe = jnp.concatenate([halo_refs[b][...], x], axis=0)
        halo_refs[b][...] = x[ck - SUBLANES:ck]
        y = x * cw[DN_CONV - 1:DN_CONV]
        for s in range(1, DN_CONV):
            y = y + xe[SUBLANES - s:SUBLANES - s + ck] * cw[DN_CONV - 1 - s:DN_CONV - s]
        ys.append(y * jax.nn.sigmoid(y))
        ba = ba_ref[b]
        betas.append(jax.nn.sigmoid(ba))
        gs_all.append(neg_a * _softplus(ba + dtb))

    qs, ks, kbs, vbs, gbs = [], [], [], [], []
    for b, h in chains:
        y = ys[b]
        q = y[:, h * DN_DIM:(h + 1) * DN_DIM]
        k = y[:, DN_W + h * DN_DIM:DN_W + (h + 1) * DN_DIM]
        v = y[:, 2 * DN_W + h * DN_DIM:2 * DN_W + (h + 1) * DN_DIM]
        q = q * lax.rsqrt(jnp.sum(q * q, axis=-1, keepdims=True) + EPS) * (DN_DIM ** -0.5)
        k = k * lax.rsqrt(jnp.sum(k * k, axis=-1, keepdims=True) + EPS)
        beta = betas[b][:, h:h + 1]
        qs.append(q)
        ks.append(k)
        kbs.append(k * beta)
        vbs.append(v * beta)
        gbs.append(jnp.broadcast_to(gs_all[b][:, DN_HEADS + h:DN_HEADS + h + 1], (ck, ck)))
    nc = len(chains)
    gc_i = [_dot_exact_lhs(tril_b, gb) for gb in gbs]
    kq = [lax.dot_general(jnp.concatenate([kbs[c], qs[c]], axis=0).astype(BF16), ks[c].astype(BF16), NT_DIMS,
                          preferred_element_type=F32) for c in range(nc)]
    gcols, glasts, intras, t_mats, pws = [], [], [], [], []
    for c in range(nc):
        gc_j = gc_i[c].T
        decay = jnp.where(tril, jnp.exp(jnp.where(tril, gc_i[c] - gc_j, 0.0)), 0.0)
        gcols.append(gc_i[c][:, 0:1])
        glasts.append(gc_j[:, ck - 1:ck])
        a = -jnp.where(strict, kq[c][:ck] * decay, 0.0)
        intras.append(jnp.where(tril, kq[c][ck:] * decay, 0.0))
        t_mats.append(eye_f + a)
        pws.append(a)
    pws = [_dot_hi(p, p) for p in pws]
    for _ in range(4):
        both = [_dot_hi(pws[c], jnp.concatenate([t_mats[c], pws[c]], axis=1)) for c in range(nc)]
        t_mats = [t_mats[c] + both[c][:, :ck] for c in range(nc)]
        pws = [both[c][:, ck:] for c in range(nc)]
    t_mats = [t_mats[c] + _dot_hi(pws[c], t_mats[c]) for c in range(nc)]
    egcs = [jnp.exp(g) for g in gcols]
    uw = [_bdot(t_mats[c], jnp.concatenate([vbs[c], kbs[c] * egcs[c]], axis=1)) for c in range(nc)]
    sts = [s_refs[c][...] for c in range(nc)]
    ws = [_bdot(jnp.concatenate([uw[c][:, DN_DIM:], qs[c] * egcs[c]], axis=0), sts[c]) for c in range(nc)]
    v_new = [uw[c][:, :DN_DIM] - ws[c][:ck] for c in range(nc)]
    o_intra = [_bdot(intras[c], v_new[c]) for c in range(nc)]
    s_add = [lax.dot_general((ks[c] * jnp.exp(glasts[c] - gcols[c])).astype(BF16), v_new[c].astype(BF16), TN_DIMS,
                             preferred_element_type=F32) for c in range(nc)]
    for c, (b, h) in enumerate(chains):
        egl = jnp.exp(jnp.concatenate([glasts[c], glasts[c]], axis=0))
        s_refs[c][...] = sts[c] * egl + s_add[c]
        o = ws[c][ck:] + o_intra[c]
        on = o * lax.rsqrt(jnp.mean(o * o, axis=-1, keepdims=True) + EPS) * onw
        gt = gate_ref[b, :, h * DN_DIM:(h + 1) * DN_DIM]
        o_ref[b, :, h * DN_DIM:(h + 1) * DN_DIM] = (on * (gt * jax.nn.sigmoid(gt))).astype(o_ref.dtype)


def _lane_row(vals, offset):
    return jnp.zeros((1, LANES), F32).at[0, offset:offset + vals.shape[0]].set(vals)


def delta_net(dqkv, gate, ba, conv_w, a_log, dt_bias, out_norm):
    b, t, _ = dqkv.shape
    ck = DN_CHUNK
    blk = lambda c: (0, c, 0)
    const = lambda c: (0, 0)
    return pl.pallas_call(
        _dn_kernel,
        grid=(t // ck,),
        in_specs=[pl.BlockSpec((b, ck, DN_QKV_W), blk), pl.BlockSpec((b, ck, DN_W), blk),
                  pl.BlockSpec((b, ck, LANES), blk),
                  pl.BlockSpec((DN_CONV, DN_QKV_W), const),
                  pl.BlockSpec((1, LANES), const), pl.BlockSpec((1, LANES), const),
                  pl.BlockSpec((1, DN_DIM), const)],
        out_specs=pl.BlockSpec((b, ck, DN_W), blk),
        out_shape=jax.ShapeDtypeStruct((b, t, DN_W), BF16),
        scratch_shapes=([pltpu.VMEM((SUBLANES, DN_QKV_W), F32)] * b
                        + [pltpu.VMEM((DN_DIM, DN_DIM), F32)] * (b * DN_HEADS)),
        compiler_params=_params("arbitrary"),
        name="gated_delta_net",
    )(dqkv, gate, ba, conv_w, _lane_row(a_log, DN_HEADS), _lane_row(dt_bias, DN_HEADS),
      out_norm.reshape(1, DN_DIM))


def _mix_kernel(att_ref, dn_ref, x_ref, wa_ref, wd_ref, n2_ref, wq_ref, x1_ref, h2_ref, qry_ref):
    x1 = (x_ref[...] + jnp.dot(att_ref[...], wa_ref[...], preferred_element_type=F32)
          + jnp.dot(dn_ref[...], wd_ref[...], preferred_element_type=F32))
    x1_ref[...] = x1
    h2 = (x1 * lax.rsqrt(jnp.mean(x1 * x1, axis=-1, keepdims=True) + EPS) * n2_ref[...]).astype(BF16)
    h2_ref[...] = h2
    qry_ref[...] = jnp.dot(h2, wq_ref[...], preferred_element_type=F32)


def mix_proj(att, dn, x2, w_out, norm2_w, w_query, tm=512):
    n, d = x2.shape
    wa = w_out[:ATT_Q_W].astype(BF16)
    wd = w_out[ATT_Q_W:].astype(BF16)
    wq = w_query.astype(BF16)
    qw = wq.shape[1]
    row = lambda i: (i, 0)
    const = lambda i: (0, 0)
    return pl.pallas_call(
        _mix_kernel,
        grid=(n // tm,),
        in_specs=[pl.BlockSpec((tm, ATT_Q_W), row), pl.BlockSpec((tm, DN_W), row), pl.BlockSpec((tm, d), row),
                  pl.BlockSpec(wa.shape, const), pl.BlockSpec(wd.shape, const),
                  pl.BlockSpec((1, d), const), pl.BlockSpec(wq.shape, const)],
        out_specs=[pl.BlockSpec((tm, d), row), pl.BlockSpec((tm, d), row), pl.BlockSpec((tm, qw), row)],
        out_shape=[jax.ShapeDtypeStruct((n, d), F32), jax.ShapeDtypeStruct((n, d), BF16),
                   jax.ShapeDtypeStruct((n, qw), F32)],
        compiler_params=_params("parallel"),
        name="mix_proj",
    )(att, dn, x2, wa, wd, norm2_w.reshape(1, d), wq)


def _merge_exchange_pairs(n):
    pairs = []
    t = max(1, (n - 1).bit_length())
    p = 1 << (t - 1)
    while p > 0:
        q, r, d = 1 << (t - 1), 0, p
        while d > 0:
            pairs += [(i, i + d) for i in range(n - d) if (i & p) == r]
            d, q, r = q - p, q >> 1, p
        p >>= 1
    return pairs


def _sort_desc(xs):
    xs = list(xs)
    for a, b in _merge_exchange_pairs(len(xs)):
        xs[a], xs[b] = jnp.maximum(xs[a], xs[b]), jnp.minimum(xs[a], xs[b])
    return xs


def _bitonic_merge_desc(xs):
    xs = list(xs)
    n = len(xs)
    d = n // 2
    while d > 0:
        for i in range(n):
            if (i & d) == 0:
                xs[i], xs[i + d] = jnp.maximum(xs[i], xs[i + d]), jnp.minimum(xs[i], xs[i + d])
        d //= 2
    return xs


def _top_of_union(xs, shift):
    k = len(xs)
    rolled = [None if x is None else pltpu.roll(x, shift, 0) for x in xs]
    out = []
    for i in range(k):
        a, b = xs[i], rolled[k - 1 - i]
        out.append(b if a is None else a if b is None else jnp.maximum(a, b))
    return out


def _top_sorted_all_sublanes(slabs):
    xs = _sort_desc(slabs)[:PEER_TOPK]
    xs += [None] * (PEER_TOPK - len(xs))
    for shift in (4, 2):
        xs = _bitonic_merge_desc(_top_of_union(xs, shift))
    return _top_of_union(xs, 1)


def _pack_sublanes(vals, sub):
    out = vals[0]
    for s in range(1, SUBLANES):
        out = jnp.where(sub == s, vals[s], out)
    return out


def _route_kernel(q_ref, k1_ref, k2_ref, cnt_ref, e1_ref, rk_ref, e2_ref):
    q = q_ref[...].astype(BF16)
    s1 = lax.dot_general(k1_ref[...], q[:, :PEER_HALF], NT_DIMS, preferred_element_type=F32)
    s2 = lax.dot_general(k2_ref[...], q[:, PEER_HALF:], NT_DIMS, preferred_element_type=F32)
    nslab = PEER_KEYS // SUBLANES
    s1s = [s1[SUBLANES * v:SUBLANES * (v + 1)] for v in range(nslab)]
    s2s = [s2[SUBLANES * v:SUBLANES * (v + 1)] for v in range(nslab)]
    v1 = _bitonic_merge_desc(_top_sorted_all_sublanes(s1s))
    v2 = _bitonic_merge_desc(_top_sorted_all_sublanes(s2s))
    sub = lax.broadcasted_iota(jnp.int32, v1[0].shape, 0)
    v2a, v2b = _pack_sublanes(v2[:SUBLANES], sub), _pack_sublanes(v2[SUBLANES:], sub)
    v1b = _pack_sublanes(v1[SUBLANES:], sub)
    cands = [v1[0] + v2a, v1[0] + v2b] + [v1[i] + v2a for i in range(1, SUBLANES)] + [v1b + v2[0]]
    top16 = _top_sorted_all_sublanes(cands)
    tau = top16[0]
    for x in top16[1:]:
        tau = jnp.minimum(tau, x)
    top = v1[0] + v2[0]
    z = None
    for c in cands:
        term = jnp.where(c >= tau, jnp.exp(c - top), 0.0)
        z = term if z is None else z + term
    for shift in (4, 2, 1):
        z = z + pltpu.roll(z, shift, 0)
    inv_z = 1.0 / z
    dense_r = 4
    cnt_rank = []
    for i, jmax in ((0, 16), (1, 8), (2, 5)):
        t = jnp.zeros_like(tau)
        for r in range(dense_r, jmax):
            t = t + jnp.where(v1[i] + v2[r] >= tau, 1.0, 0.0)
        cnt_rank.append(t)
    rks, e2s = [], []
    for v in range(nslab):
        rows = slice(SUBLANES * v, SUBLANES * (v + 1))
        t = jnp.zeros_like(tau)
        for r in range(dense_r):
            t = t + jnp.where(s1s[v] + v2[r] >= tau, 1.0, 0.0)
        extra = jnp.zeros_like(tau)
        for i in range(len(cnt_rank)):
            extra = jnp.where(s1s[v] == v1[i], jnp.maximum(extra, cnt_rank[i]), extra)
        cnt_ref[0, rows] = t + extra
        e1_ref[0, rows] = jnp.exp(s1s[v] - v1[0]) * inv_z
        rk = jnp.zeros_like(tau)
        for r in range(PEER_TOPK):
            rk = rk + jnp.where(v2[r] > s2s[v], 1.0, 0.0)
        rks.append(rk)
        e2s.append(jnp.exp(s2s[v] - v2[0]))
    pack = 2 * SUBLANES
    for v in range(0, nslab, 2):
        rows = slice(SUBLANES * v, SUBLANES * v + pack)
        rk_ref[rows] = jnp.concatenate(rks[v:v + 2], axis=0).astype(BF16)
        e2_ref[rows] = jnp.concatenate(e2s[v:v + 2], axis=0).astype(BF16)


def peer_route(qry, keys1, keys2, tn=256):
    n = qry.shape[0]
    qblk = 2 * PEER_HALF
    out = pl.BlockSpec((1, PEER_KEYS, tn), lambda i, h: (h, 0, i))
    out2 = pl.BlockSpec((PEER_KEYS, tn), lambda i, h: (h, i))
    const = lambda i, h: (0, 0)
    return pl.pallas_call(
        _route_kernel,
        grid=(n // tn, PEER_HEADS),
        in_specs=[pl.BlockSpec((tn, qblk), lambda i, h: (i, h)),
                  pl.BlockSpec((PEER_KEYS, PEER_HALF), const), pl.BlockSpec((PEER_KEYS, PEER_HALF), const)],
        out_specs=[out, out, out2, out2],
        out_shape=[jax.ShapeDtypeStruct((PEER_HEADS, PEER_KEYS, n), F32)] * 2
                  + [jax.ShapeDtypeStruct((PEER_HEADS * PEER_KEYS, n), BF16)] * 2,
        compiler_params=_params("parallel", "parallel"),
        name="peer_route",
    )(qry, keys1.astype(BF16), keys2.astype(BF16))


PEER_EXPERT_BLOCK = 1024
PEER_TOKEN_TILE = 512
GATE_ROWS = 64
GATE_KEYS = 4


def _expert_kernel(h_ref, x1_ref, wd_ref, wu_prev_ref, wu_last_ref, cnt_ref, e1_ref, rk_in_ref, e2_in_ref, o_ref,
                   act_ref, g_ref, hw_ref, acc_ref, rk_ref, e2_ref):
    e = pl.program_id(1)
    tn = h_ref.shape[0]

    @pl.when(e == 0)
    def _():
        acc_ref[...] = jnp.zeros_like(acc_ref)
        hw_ref[...] = jnp.zeros_like(hw_ref)
        rk_ref[...] = rk_in_ref[...]
        e2_ref[...] = e2_in_ref[...]

    acc_ref[...] += jnp.dot(wu_prev_ref[...], hw_ref[...], preferred_element_type=F32)
    act_ref[...] = lax.dot_general(wd_ref[...], h_ref[...], NT_DIMS, preferred_element_type=F32)
    pack = 2 * SUBLANES
    for lt in range(tn // LANES):
        ls = slice(lt * LANES, (lt + 1) * LANES)
        for jb in range(PEER_KEYS // GATE_ROWS):
            js = slice(jb * GATE_ROWS, (jb + 1) * GATE_ROWS)
            for ab in range(PEER_EXPERT_BLOCK // PEER_KEYS // GATE_KEYS):
                keys = range(ab * GATE_KEYS, (ab + 1) * GATE_KEYS)
                gs = [jnp.zeros((GATE_ROWS, LANES), BF16) for _ in keys]
                for hd in range(PEER_HEADS):
                    hj = slice(hd * PEER_KEYS + jb * GATE_ROWS, hd * PEER_KEYS + (jb + 1) * GATE_ROWS)
                    rk = rk_ref[hj, ls]
                    e2 = e2_ref[hj, ls]
                    for n, al in enumerate(keys):
                        cnt = jnp.broadcast_to(cnt_ref[hd, al:al + 1, ls], (pack, LANES)).astype(BF16)
                        e1 = jnp.broadcast_to(e1_ref[hd, al:al + 1, ls], (pack, LANES)).astype(BF16)
                        cnt = jnp.concatenate([cnt] * (GATE_ROWS // pack), axis=0)
                        e1 = jnp.concatenate([e1] * (GATE_ROWS // pack), axis=0)
                        gs[n] = gs[n] + jnp.where(rk < cnt, e2 * e1, jnp.zeros_like(e2))
                for n, al in enumerate(keys):
                    g_ref[al * PEER_KEYS + jb * GATE_ROWS:al * PEER_KEYS + (jb + 1) * GATE_ROWS, ls] = gs[n]
    for r0 in range(0, PEER_EXPERT_BLOCK, GATE_ROWS):
        z = act_ref[r0:r0 + GATE_ROWS, :]
        gelu = 0.5 * z * (1.0 + lax.erf(z * (2.0 ** -0.5)))
        hw_ref[r0:r0 + GATE_ROWS, :] = g_ref[r0:r0 + GATE_ROWS, :] * gelu.astype(BF16)

    @pl.when(e == pl.num_programs(1) - 1)
    def _():
        acc = acc_ref[...] + jnp.dot(wu_last_ref[...], hw_ref[...], preferred_element_type=F32)
        o_ref[...] = x1_ref[...] + acc.T


def peer_experts(h2, x1, expert_down, expert_up, cnt, e1, rk, e2):
    n, d = h2.shape
    ne = expert_down.shape[0]
    tn, eb = PEER_TOKEN_TILE, PEER_EXPERT_BLOCK
    ab = eb // PEER_KEYS
    wd = expert_down.astype(BF16)
    wu_t = expert_up.astype(BF16).T
    tok = lambda i, e: (i, 0)
    full = pl.BlockSpec((PEER_HEADS * PEER_KEYS, tn), lambda i, e: (0, i))
    part = pl.BlockSpec((PEER_HEADS, ab, tn), lambda i, e: (0, e, i))
    return pl.pallas_call(
        _expert_kernel,
        grid=(n // tn, ne // eb),
        in_specs=[pl.BlockSpec((tn, d), tok), pl.BlockSpec((tn, d), tok),
                  pl.BlockSpec((eb, d), lambda i, e: (e, 0)),
                  pl.BlockSpec((d, eb), lambda i, e: (0, jnp.maximum(e - 1, 0))),
                  pl.BlockSpec((d, eb), lambda i, e: (0, ne // eb - 1)),
                  part, part, full, full],
        out_specs=pl.BlockSpec((tn, d), tok),
        out_shape=jax.ShapeDtypeStruct((n, d), F32),
        scratch_shapes=[pltpu.VMEM((eb, tn), F32), pltpu.VMEM((eb, tn), BF16), pltpu.VMEM((eb, tn), BF16),
                        pltpu.VMEM((d, tn), F32),
                        pltpu.VMEM((PEER_HEADS * PEER_KEYS, tn), BF16), pltpu.VMEM((PEER_HEADS * PEER_KEYS, tn), BF16)],
        compiler_params=_params("parallel", "arbitrary"),
        name="peer_experts",
    )(h2, x1, wd, wu_t, wu_t, cnt, e1, rk, e2)


def kernel(x, norm1_w, w_in, att_q_norm, att_k_norm, att_sinks, dn_conv_w, dn_A_log, dn_dt_bias, dn_out_norm, w_out, norm2_w, peer_w_query, peer_sub_keys1, peer_sub_keys2, peer_expert_down, peer_expert_up):
    b, t, d = x.shape
    x2 = x.reshape(b * t, d)
    for l in range(norm1_w.shape[0]):
        a_q, a_k, a_v, d_qkv, d_gate, d_ba = in_proj(x2, norm1_w[l], w_in[l])
        seq = lambda a: a.reshape(b, t, a.shape[-1])
        att = attention(seq(a_q), seq(a_k), seq(a_v), att_q_norm[l], att_k_norm[l], att_sinks[l])
        dn = delta_net(seq(d_qkv), seq(d_gate), seq(d_ba), dn_conv_w[l], dn_A_log[l], dn_dt_bias[l],
                       dn_out_norm[l])
        x1, h2, qry = mix_proj(att.reshape(b * t, -1), dn.reshape(b * t, -1), x2, w_out[l], norm2_w[l],
                               peer_w_query[l])
        cnt, e1, rk, e2 = peer_route(qry, peer_sub_keys1[l], peer_sub_keys2[l])
        x2 = peer_experts(h2, x1, peer_expert_down[l], peer_expert_up[l], cnt, e1, rk, e2)
    return x2.reshape(b, t, d)
```

```python
import jax
import jax.numpy as jnp
from jax import lax
from jax.experimental import pallas as pl
from jax.experimental.pallas import tpu as pltpu

EPS = 1e-6
LANES = 128
SUBLANES = 8
VMEM_LIMIT = 56 * 1024 * 1024

ATT_HEADS, ATT_KV_HEADS, ATT_HEAD_DIM = 8, 2, 64
ATT_GROUP = ATT_HEADS // ATT_KV_HEADS
ATT_BLOCK = 128
ROPE_THETA = 500000.0
ROPE_DIM = ATT_HEAD_DIM // 4
DN_HEADS, DN_DIM, DN_CONV, DN_CHUNK = 4, 128, 4, 64
ATT_Q_W = ATT_HEADS * ATT_HEAD_DIM
ATT_KV_W = ATT_KV_HEADS * ATT_HEAD_DIM
DN_W = DN_HEADS * DN_DIM
DN_QKV_W = 3 * DN_W
IN_MAIN_W = ATT_Q_W + 2 * ATT_KV_W + DN_QKV_W + DN_W
PEER_HEADS, PEER_KEYS, PEER_TOPK = 8, 128, 16
PEER_HALF = 128

F32 = jnp.float32
BF16 = jnp.bfloat16
NT_DIMS = (((1,), (1,)), ((), ()))
TN_DIMS = (((0,), (0,)), ((), ()))


def _params(*sem):
    return pltpu.CompilerParams(dimension_semantics=sem, vmem_limit_bytes=VMEM_LIMIT)


def _bdot(a, b):
    return jnp.dot(a.astype(BF16), b.astype(BF16), preferred_element_type=F32)


def _hi_lo(x):
    hi = x.astype(BF16)
    return hi, (x - hi.astype(F32)).astype(BF16)


def _dot_hi(a, b):
    ah, al = _hi_lo(a)
    bh, bl = _hi_lo(b)
    return (jnp.dot(ah, bh, preferred_element_type=F32) + jnp.dot(ah, bl, preferred_element_type=F32)
            + jnp.dot(al, bh, preferred_element_type=F32))


def _dot_exact_lhs(lhs, x):
    hi = x.astype(BF16)
    r = x - hi.astype(F32)
    mid = r.astype(BF16)
    lo = (r - mid.astype(F32)).astype(BF16)
    return (jnp.dot(lhs, hi, preferred_element_type=F32) + jnp.dot(lhs, mid, preferred_element_type=F32)
            + jnp.dot(lhs, lo, preferred_element_type=F32))


def _in_proj_kernel(x_ref, nw_ref, w_ref, q_ref, k_ref, v_ref, dqkv_ref, gate_ref, ba_ref):
    x = x_ref[...]
    h = x * lax.rsqrt(jnp.mean(x * x, axis=-1, keepdims=True) + EPS) * nw_ref[...]
    p = jnp.dot(h.astype(BF16), w_ref[...], preferred_element_type=F32)
    o = 0
    for ref in (q_ref, k_ref, v_ref, dqkv_ref, gate_ref, ba_ref):
        w = ref.shape[-1]
        ref[...] = p[:, o:o + w]
        o += w


def in_proj(x2, norm_w, w_in, tm=512):
    n, d = x2.shape
    small = w_in.shape[1] - IN_MAIN_W
    w_pad = jnp.pad(w_in, ((0, 0), (0, LANES - small))).astype(BF16)
    widths = (ATT_Q_W, ATT_KV_W, ATT_KV_W, DN_QKV_W, DN_W, LANES)
    return pl.pallas_call(
        _in_proj_kernel,
        grid=(n // tm,),
        in_specs=[pl.BlockSpec((tm, d), lambda i: (i, 0)),
                  pl.BlockSpec((1, d), lambda i: (0, 0)),
                  pl.BlockSpec(w_pad.shape, lambda i: (0, 0))],
        out_specs=[pl.BlockSpec((tm, w), lambda i: (i, 0)) for w in widths],
        out_shape=[jax.ShapeDtypeStruct((n, w), F32) for w in widths],
        compiler_params=_params("parallel"),
        name="in_proj",
    )(x2, norm_w.reshape(1, d), w_pad)


def _norm_rope(x, w, c, sm, sp):
    rows, width = x.shape
    x2 = x * x
    parts = []
    for h in range(width // ATT_HEAD_DIM):
        ms = jnp.mean(x2[:, h * ATT_HEAD_DIM:(h + 1) * ATT_HEAD_DIM], axis=-1, keepdims=True)
        parts.append(jnp.broadcast_to(lax.rsqrt(ms + EPS), (rows, ATT_HEAD_DIM)))
    xn = x * jnp.concatenate(parts, axis=1) * w
    half = ROPE_DIM // 2
    return xn * c + pltpu.roll(xn, width - half, 1) * sm + pltpu.roll(xn, half, 1) * sp


def _attn_kernel(q_ref, kc_ref, kp_ref, vc_ref, vp_ref, c_ref, sm_ref, sp_ref,
                 cp_ref, smp_ref, spp_ref, qw_ref, kw_ref, sink_ref, o_ref):
    i = pl.program_id(1)
    blk = ATT_BLOCK
    q = _norm_rope(q_ref[0], qw_ref[...], c_ref[...], sm_ref[...], sp_ref[...])
    kw = kw_ref[...]
    kc = _norm_rope(kc_ref[0], kw, c_ref[:, :ATT_KV_W], sm_ref[:, :ATT_KV_W], sp_ref[:, :ATT_KV_W])
    kp = _norm_rope(kp_ref[0], kw, cp_ref[...], smp_ref[...], spp_ref[...])
    vc = vc_ref[0]
    vp = vp_ref[0]
    rows = ATT_GROUP * blk
    qi = lax.broadcasted_iota(jnp.int32, (rows, blk), 0) % blk
    kj = lax.broadcasted_iota(jnp.int32, (rows, blk), 1)
    mask_c = kj <= qi
    mask_p = jnp.logical_and(kj > qi, i > 0)
    scale = ATT_HEAD_DIM ** -0.5
    outs = []
    for g in range(ATT_KV_HEADS):
        heads = range(g * ATT_GROUP, (g + 1) * ATT_GROUP)
        qg = jnp.concatenate([q[:, h * ATT_HEAD_DIM:(h + 1) * ATT_HEAD_DIM] for h in heads], axis=0)
        sink = jnp.concatenate([jnp.broadcast_to(sink_ref[h:h + 1, 0:1], (blk, 1)) for h in heads], axis=0)
        lo, hi = g * ATT_HEAD_DIM, (g + 1) * ATT_HEAD_DIM
        qb = qg.astype(BF16)
        s_c = lax.dot_general(qb, kc[:, lo:hi].astype(BF16), NT_DIMS, preferred_element_type=F32) * scale
        s_p = lax.dot_general(qb, kp[:, lo:hi].astype(BF16), NT_DIMS, preferred_element_type=F32) * scale
        s_c = jnp.where(mask_c, s_c, -jnp.inf)
        s_p = jnp.where(mask_p, s_p, -jnp.inf)
        m = jnp.maximum(jnp.maximum(jnp.max(s_c, axis=-1, keepdims=True),
                                    jnp.max(s_p, axis=-1, keepdims=True)), sink)
        p_c = jnp.exp(s_c - m)
        p_p = jnp.exp(s_p - m)
        denom = (jnp.sum(p_c, axis=-1, keepdims=True) + jnp.sum(p_p, axis=-1, keepdims=True)
                 + jnp.exp(sink - m))
        og = (_bdot(p_c, vc[:, lo:hi]) + _bdot(p_p, vp[:, lo:hi])) / denom
        outs += [og[j * blk:(j + 1) * blk] for j in range(ATT_GROUP)]
    o_ref[0] = jnp.concatenate(outs, axis=1).astype(o_ref.dtype)


def _rope_tables(t):
    half = ROPE_DIM // 2
    inv_freq = ROPE_THETA ** (-(jnp.arange(half, dtype=F32) * 2.0 / ROPE_DIM))
    ang = jnp.arange(t).astype(F32)[:, None] * inv_freq[None, :]
    cos, sin = jnp.cos(ang), jnp.sin(ang)
    rest = ATT_HEAD_DIM - ROPE_DIM
    c = jnp.concatenate([cos, cos, jnp.ones((t, rest), F32)], axis=1)
    sm = jnp.concatenate([-sin, jnp.zeros((t, half + rest), F32)], axis=1)
    sp = jnp.concatenate([jnp.zeros((t, half), F32), sin, jnp.zeros((t, rest), F32)], axis=1)
    return tuple(jnp.tile(a, (1, ATT_HEADS)) for a in (c, sm, sp))


def attention(q, k, v, q_norm, k_norm, sinks):
    b, t, _ = q.shape
    blk = ATT_BLOCK
    c, sm, sp = _rope_tables(t)
    cur = lambda bi, i: (bi, i, 0)
    prev = lambda bi, i: (bi, jnp.maximum(i - 1, 0), 0)
    tcur = lambda bi, i: (i, 0)
    tprev = lambda bi, i: (jnp.maximum(i - 1, 0), 0)
    const = lambda bi, i: (0, 0)
    return pl.pallas_call(
        _attn_kernel,
        grid=(b, t // blk),
        in_specs=[pl.BlockSpec((1, blk, ATT_Q_W), cur),
                  pl.BlockSpec((1, blk, ATT_KV_W), cur), pl.BlockSpec((1, blk, ATT_KV_W), prev),
                  pl.BlockSpec((1, blk, ATT_KV_W), cur), pl.BlockSpec((1, blk, ATT_KV_W), prev),
                  pl.BlockSpec((blk, ATT_Q_W), tcur), pl.BlockSpec((blk, ATT_Q_W), tcur),
                  pl.BlockSpec((blk, ATT_Q_W), tcur),
                  pl.BlockSpec((blk, ATT_KV_W), tprev), pl.BlockSpec((blk, ATT_KV_W), tprev),
                  pl.BlockSpec((blk, ATT_KV_W), tprev),
                  pl.BlockSpec((1, ATT_Q_W), const), pl.BlockSpec((1, ATT_KV_W), const),
                  pl.BlockSpec((ATT_HEADS, LANES), const)],
        out_specs=pl.BlockSpec((1, blk, ATT_Q_W), cur),
        out_shape=jax.ShapeDtypeStruct((b, t, ATT_Q_W), BF16),
        compiler_params=_params("parallel", "parallel"),
        name="swa_attention",
    )(q, k, k, v, v, c, sm, sp, c, sm, sp,
      jnp.tile(q_norm, ATT_HEADS).reshape(1, ATT_Q_W), jnp.tile(k_norm, ATT_KV_HEADS).reshape(1, ATT_KV_W),
      jnp.broadcast_to(sinks[:, None], (ATT_HEADS, LANES)))


def _softplus(x):
    return jnp.maximum(x, 0.0) + jnp.log1p(jnp.exp(-jnp.abs(x)))


def _dn_kernel(x_ref, gate_ref, ba_ref, cw_ref, alog_ref, dtb_ref, onw_ref, o_ref, *scratch):
    ck = DN_CHUNK
    nb = x_ref.shape[0]
    halo_refs = scratch[:nb]
    s_refs = scratch[nb:]

    @pl.when(pl.program_id(0) == 0)
    def _():
        for ref in scratch:
            ref[...] = jnp.zeros_like(ref)

    cw = cw_ref[...]
    onw = onw_ref[...]
    neg_a = -jnp.exp(alog_ref[...])
    dtb = dtb_ref[...]
    ri = lax.broadcasted_iota(jnp.int32, (ck, ck), 0)
    ci = lax.broadcasted_iota(jnp.int32, (ck, ck), 1)
    tril = ri >= ci
    strict = ri > ci
    tril_b = jnp.where(tril, 1.0, 0.0).astype(BF16)
    eye_f = jnp.where(ri == ci, 1.0, 0.0).astype(F32)

    chains = [(b, h) for b in range(nb) for h in range(DN_HEADS)]
    ys, betas, gs_all = [], [], []
    for b in range(nb):
        x = x_ref[b]
        xe = jnp.concatenate([halo_refs[b][...], x], axis=0)
        halo_refs[b][...] = x[ck - SUBLANES:ck]
        y = x * cw[DN_CONV - 1:DN_CONV]
        for s in range(1, DN_CONV):
            y = y + xe[SUBLANES - s:SUBLANES - s + ck] * cw[DN_CONV - 1 - s:DN_CONV - s]
        ys.append(y * jax.nn.sigmoid(y))
        ba = ba_ref[b]
        betas.append(jax.nn.sigmoid(ba))
        gs_all.append(neg_a * _softplus(ba + dtb))

    qs, ks, kbs, vbs, gbs = [], [], [], [], []
    for b, h in chains:
        y = ys[b]
        q = y[:, h * DN_DIM:(h + 1) * DN_DIM]
        k = y[:, DN_W + h * DN_DIM:DN_W + (h + 1) * DN_DIM]
        v = y[:, 2 * DN_W + h * DN_DIM:2 * DN_W + (h + 1) * DN_DIM]
        q = q * lax.rsqrt(jnp.sum(q * q, axis=-1, keepdims=True) + EPS) * (DN_DIM ** -0.5)
        k = k * lax.rsqrt(jnp.sum(k * k, axis=-1, keepdims=True) + EPS)
        beta = betas[b][:, h:h + 1]
        qs.append(q)
        ks.append(k)
        kbs.append(k * beta)
        vbs.append(v * beta)
        gbs.append(jnp.broadcast_to(gs_all[b][:, DN_HEADS + h:DN_HEADS + h + 1], (ck, ck)))
    nc = len(chains)
    gc_i = [_dot_exact_lhs(tril_b, gb) for gb in gbs]
    kq = [lax.dot_general(jnp.concatenate([kbs[c], qs[c]], axis=0).astype(BF16), ks[c].astype(BF16), NT_DIMS,
                          preferred_element_type=F32) for c in range(nc)]
    gcols, glasts, intras, t_mats, pws = [], [], [], [], []
    for c in range(nc):
        gc_j = gc_i[c].T
        decay = jnp.where(tril, jnp.exp(jnp.where(tril, gc_i[c] - gc_j, 0.0)), 0.0)
        gcols.append(gc_i[c][:, 0:1])
        glasts.append(gc_j[:, ck - 1:ck])
        a = -jnp.where(strict, kq[c][:ck] * decay, 0.0)
        intras.append(jnp.where(tril, kq[c][ck:] * decay, 0.0))
        t_mats.append(eye_f + a)
        pws.append(a)
    pws = [_dot_hi(p, p) for p in pws]
    for _ in range(4):
        both = [_dot_hi(pws[c], jnp.concatenate([t_mats[c], pws[c]], axis=1)) for c in range(nc)]
        t_mats = [t_mats[c] + both[c][:, :ck] for c in range(nc)]
        pws = [both[c][:, ck:] for c in range(nc)]
    t_mats = [t_mats[c] + _dot_hi(pws[c], t_mats[c]) for c in range(nc)]
    egcs = [jnp.exp(g) for g in gcols]
    uw = [_bdot(t_mats[c], jnp.concatenate([vbs[c], kbs[c] * egcs[c]], axis=1)) for c in range(nc)]
    sts = [s_refs[c][...] for c in range(nc)]
    ws = [_bdot(jnp.concatenate([uw[c][:, DN_DIM:], qs[c] * egcs[c]], axis=0), sts[c]) for c in range(nc)]
    v_new = [uw[c][:, :DN_DIM] - ws[c][:ck] for c in range(nc)]
    o_intra = [_bdot(intras[c], v_new[c]) for c in range(nc)]
    s_add = [lax.dot_general((ks[c] * jnp.exp(glasts[c] - gcols[c])).astype(BF16), v_new[c].astype(BF16), TN_DIMS,
                             preferred_element_type=F32) for c in range(nc)]
    for c, (b, h) in enumerate(chains):
        egl = jnp.exp(jnp.concatenate([glasts[c], glasts[c]], axis=0))
        s_refs[c][...] = sts[c] * egl + s_add[c]
        o = ws[c][ck:] + o_intra[c]
        on = o * lax.rsqrt(jnp.mean(o * o, axis=-1, keepdims=True) + EPS) * onw
        gt = gate_ref[b, :, h * DN_DIM:(h + 1) * DN_DIM]
        o_ref[b, :, h * DN_DIM:(h + 1) * DN_DIM] = (on * (gt * jax.nn.sigmoid(gt))).astype(o_ref.dtype)


def _lane_row(vals, offset):
    return jnp.zeros((1, LANES), F32).at[0, offset:offset + vals.shape[0]].set(vals)


def delta_net(dqkv, gate, ba, conv_w, a_log, dt_bias, out_norm):
    b, t, _ = dqkv.shape
    ck = DN_CHUNK
    blk = lambda c: (0, c, 0)
    const = lambda c: (0, 0)
    return pl.pallas_call(
        _dn_kernel,
        grid=(t // ck,),
        in_specs=[pl.BlockSpec((b, ck, DN_QKV_W), blk), pl.BlockSpec((b, ck, DN_W), blk),
                  pl.BlockSpec((b, ck, LANES), blk),
                  pl.BlockSpec((DN_CONV, DN_QKV_W), const),
                  pl.BlockSpec((1, LANES), const), pl.BlockSpec((1, LANES), const),
                  pl.BlockSpec((1, DN_DIM), const)],
        out_specs=pl.BlockSpec((b, ck, DN_W), blk),
        out_shape=jax.ShapeDtypeStruct((b, t, DN_W), BF16),
        scratch_shapes=([pltpu.VMEM((SUBLANES, DN_QKV_W), F32)] * b
                        + [pltpu.VMEM((DN_DIM, DN_DIM), F32)] * (b * DN_HEADS)),
        compiler_params=_params("arbitrary"),
        name="gated_delta_net",
    )(dqkv, gate, ba, conv_w, _lane_row(a_log, DN_HEADS), _lane_row(dt_bias, DN_HEADS),
      out_norm.reshape(1, DN_DIM))


def _mix_kernel(att_ref, dn_ref, x_ref, wa_ref, wd_ref, n2_ref, wq_ref, x1_ref, h2_ref, qry_ref):
    x1 = (x_ref[...] + jnp.dot(att_ref[...], wa_ref[...], preferred_element_type=F32)
          + jnp.dot(dn_ref[...], wd_ref[...], preferred_element_type=F32))
    x1_ref[...] = x1
    h2 = (x1 * lax.rsqrt(jnp.mean(x1 * x1, axis=-1, keepdims=True) + EPS) * n2_ref[...]).astype(BF16)
    h2_ref[...] = h2
    qry_ref[...] = jnp.dot(h2, wq_ref[...], preferred_element_type=F32)


def mix_proj(att, dn, x2, w_out, norm2_w, w_query, tm=512):
    n, d = x2.shape
    wa = w_out[:ATT_Q_W].astype(BF16)
    wd = w_out[ATT_Q_W:].astype(BF16)
    wq = w_query.astype(BF16)
    qw = wq.shape[1]
    row = lambda i: (i, 0)
    const = lambda i: (0, 0)
    return pl.pallas_call(
        _mix_kernel,
        grid=(n // tm,),
        in_specs=[pl.BlockSpec((tm, ATT_Q_W), row), pl.BlockSpec((tm, DN_W), row), pl.BlockSpec((tm, d), row),
                  pl.BlockSpec(wa.shape, const), pl.BlockSpec(wd.shape, const),
                  pl.BlockSpec((1, d), const), pl.BlockSpec(wq.shape, const)],
        out_specs=[pl.BlockSpec((tm, d), row), pl.BlockSpec((tm, d), row), pl.BlockSpec((tm, qw), row)],
        out_shape=[jax.ShapeDtypeStruct((n, d), F32), jax.ShapeDtypeStruct((n, d), BF16),
                   jax.ShapeDtypeStruct((n, qw), F32)],
        compiler_params=_params("parallel"),
        name="mix_proj",
    )(att, dn, x2, wa, wd, norm2_w.reshape(1, d), wq)


def _merge_exchange_pairs(n):
    pairs = []
    t = max(1, (n - 1).bit_length())
    p = 1 << (t - 1)
    while p > 0:
        q, r, d = 1 << (t - 1), 0, p
        while d > 0:
            pairs += [(i, i + d) for i in range(n - d) if (i & p) == r]
            d, q, r = q - p, q >> 1, p
        p >>= 1
    return pairs


def _sort_desc(xs):
    xs = list(xs)
    for a, b in _merge_exchange_pairs(len(xs)):
        xs[a], xs[b] = jnp.maximum(xs[a], xs[b]), jnp.minimum(xs[a], xs[b])
    return xs


def _bitonic_merge_desc(xs):
    xs = list(xs)
    n = len(xs)
    d = n // 2
    while d > 0:
        for i in range(n):
            if (i & d) == 0:
                xs[i], xs[i + d] = jnp.maximum(xs[i], xs[i + d]), jnp.minimum(xs[i], xs[i + d])
        d //= 2
    return xs


def _top_of_union(xs, shift):
    k = len(xs)
    rolled = [None if x is None else pltpu.roll(x, shift, 0) for x in xs]
    out = []
    for i in range(k):
        a, b = xs[i], rolled[k - 1 - i]
        out.append(b if a is None else a if b is None else jnp.maximum(a, b))
    return out


def _top_sorted_all_sublanes(slabs):
    xs = _sort_desc(slabs)[:PEER_TOPK]
    xs += [None] * (PEER_TOPK - len(xs))
    for shift in (4, 2):
        xs = _bitonic_merge_desc(_top_of_union(xs, shift))
    return _top_of_union(xs, 1)


def _pack_sublanes(vals, sub):
    out = vals[0]
    for s in range(1, SUBLANES):
        out = jnp.where(sub == s, vals[s], out)
    return out


def _route_kernel(q_ref, k1_ref, k2_ref, th_ref, e1_ref, s2_ref, e2_ref):
    q = q_ref[...].astype(BF16)
    s1 = lax.dot_general(k1_ref[...], q[:, :PEER_HALF], NT_DIMS, preferred_element_type=F32)
    s2 = lax.dot_general(k2_ref[...], q[:, PEER_HALF:], NT_DIMS, preferred_element_type=F32)
    nslab = PEER_KEYS // SUBLANES
    s1s = [s1[SUBLANES * v:SUBLANES * (v + 1)] for v in range(nslab)]
    s2s = [s2[SUBLANES * v:SUBLANES * (v + 1)] for v in range(nslab)]
    v1 = _bitonic_merge_desc(_top_sorted_all_sublanes(s1s))
    v2 = _bitonic_merge_desc(_top_sorted_all_sublanes(s2s))
    sub = lax.broadcasted_iota(jnp.int32, v1[0].shape, 0)
    v2a, v2b = _pack_sublanes(v2[:SUBLANES], sub), _pack_sublanes(v2[SUBLANES:], sub)
    v1b = _pack_sublanes(v1[SUBLANES:], sub)
    cands = [v1[0] + v2a, v1[0] + v2b] + [v1[i] + v2a for i in range(1, SUBLANES)] + [v1b + v2[0]]
    top16 = _top_sorted_all_sublanes(cands)
    tau = top16[0]
    for x in top16[1:]:
        tau = jnp.minimum(tau, x)
    top = v1[0] + v2[0]
    z = None
    for c in cands:
        term = jnp.where(c >= tau, jnp.exp(c - top), 0.0)
        z = term if z is None else z + term
    for shift in (4, 2, 1):
        z = z + pltpu.roll(z, shift, 0)
    inv_z = 1.0 / z
    dense_r = 4
    th_rank = []
    for i, jmax in ((0, 16), (1, 8), (2, 5)):
        t = jnp.full_like(tau, jnp.inf)
        for r in range(dense_r, jmax):
            t = jnp.where(v1[i] + v2[r] >= tau, v2[r], t)
        th_rank.append(t)
    for v in range(nslab):
        rows = slice(SUBLANES * v, SUBLANES * (v + 1))
        t = jnp.full_like(tau, jnp.inf)
        for r in range(dense_r):
            t = jnp.where(s1s[v] + v2[r] >= tau, v2[r], t)
        for i in range(len(th_rank)):
            t = jnp.where(s1s[v] == v1[i], jnp.minimum(t, th_rank[i]), t)
        th_ref[0, rows] = t
        e1_ref[0, rows] = jnp.exp(s1s[v] - v1[0]) * inv_z
        s2_ref[0, rows] = s2s[v]
        e2_ref[0, rows] = jnp.exp(s2s[v] - v2[0])


def peer_route(qry, keys1, keys2, tn=256):
    n = qry.shape[0]
    qblk = 2 * PEER_HALF
    out = pl.BlockSpec((1, PEER_KEYS, tn), lambda i, h: (h, 0, i))
    const = lambda i, h: (0, 0)
    return pl.pallas_call(
        _route_kernel,
        grid=(n // tn, PEER_HEADS),
        in_specs=[pl.BlockSpec((tn, qblk), lambda i, h: (i, h)),
                  pl.BlockSpec((PEER_KEYS, PEER_HALF), const), pl.BlockSpec((PEER_KEYS, PEER_HALF), const)],
        out_specs=[out] * 4,
        out_shape=[jax.ShapeDtypeStruct((PEER_HEADS, PEER_KEYS, n), F32)] * 4,
        compiler_params=_params("parallel", "parallel"),
        name="peer_route",
    )(qry, keys1.astype(BF16), keys2.astype(BF16))


PEER_EXPERT_BLOCK = 1024
PEER_TOKEN_TILE = 512
GATE_ROWS = 32
GATE_KEYS = 4


def _expert_kernel(h_ref, x1_ref, wd_ref, wu_ref, th_ref, e1_ref, s2_ref, e2_ref, o_ref, act_ref, hw_ref, acc_ref):
    e = pl.program_id(1)
    tn = h_ref.shape[0]

    @pl.when(e == 0)
    def _():
        acc_ref[...] = jnp.zeros_like(acc_ref)

    act_ref[...] = lax.dot_general(wd_ref[...], h_ref[...], NT_DIMS, preferred_element_type=F32)
    n_jb = PEER_KEYS // GATE_ROWS
    n_ab = PEER_EXPERT_BLOCK // PEER_KEYS // GATE_KEYS

    def gate_tile(t, carry):
        j0 = pl.multiple_of((t % n_jb) * GATE_ROWS, GATE_ROWS)
        ls = pl.ds(pl.multiple_of((t // n_jb) * LANES, LANES), LANES)
        for ab in range(n_ab):
            keys = range(ab * GATE_KEYS, (ab + 1) * GATE_KEYS)
            gs = [jnp.zeros((GATE_ROWS, LANES), F32) for _ in keys]
            for hd in range(PEER_HEADS):
                s2 = s2_ref[hd, pl.ds(j0, GATE_ROWS), ls]
                e2 = e2_ref[hd, pl.ds(j0, GATE_ROWS), ls]
                for n, al in enumerate(keys):
                    sel = s2 >= th_ref[hd, al:al + 1, ls]
                    gs[n] = gs[n] + jnp.where(sel, e2 * e1_ref[hd, al:al + 1, ls], 0.0)
            for n, al in enumerate(keys):
                r0 = pl.multiple_of(al * PEER_KEYS + j0, GATE_ROWS)
                z = act_ref[pl.ds(r0, GATE_ROWS), ls]
                gelu = 0.5 * z * (1.0 + lax.erf(z * (2.0 ** -0.5)))
                hw_ref[pl.ds(r0, GATE_ROWS), ls] = (gs[n] * gelu).astype(BF16)
        return carry

    lax.fori_loop(0, n_jb * (tn // LANES), gate_tile, 0)
    acc_ref[...] += jnp.dot(wu_ref[...], hw_ref[...], preferred_element_type=F32)

    @pl.when(e == pl.num_programs(1) - 1)
    def _():
        o_ref[...] = x1_ref[...] + acc_ref[...].T


def peer_experts(h2, x1, expert_down, expert_up, theta, e1, s2, e2):
    n, d = h2.shape
    ne = expert_down.shape[0]
    tn, eb = PEER_TOKEN_TILE, PEER_EXPERT_BLOCK
    ab = eb // PEER_KEYS
    wd = expert_down.astype(BF16)
    wu_t = expert_up.astype(BF16).T
    tok = lambda i, e: (i, 0)
    full = pl.BlockSpec((PEER_HEADS, PEER_KEYS, tn), lambda i, e: (0, 0, i))
    part = pl.BlockSpec((PEER_HEADS, ab, tn), lambda i, e: (0, e, i))
    return pl.pallas_call(
        _expert_kernel,
        grid=(n // tn, ne // eb),
        in_specs=[pl.BlockSpec((tn, d), tok), pl.BlockSpec((tn, d), tok),
                  pl.BlockSpec((eb, d), lambda i, e: (e, 0)), pl.BlockSpec((d, eb), lambda i, e: (0, e)),
                  part, part, full, full],
        out_specs=pl.BlockSpec((tn, d), tok),
        out_shape=jax.ShapeDtypeStruct((n, d), F32),
        scratch_shapes=[pltpu.VMEM((eb, tn), F32), pltpu.VMEM((eb, tn), BF16), pltpu.VMEM((d, tn), F32)],
        compiler_params=_params("parallel", "arbitrary"),
        name="peer_experts",
    )(h2, x1, wd, wu_t, theta, e1, s2, e2)


def kernel(x, norm1_w, w_in, att_q_norm, att_k_norm, att_sinks, dn_conv_w, dn_A_log, dn_dt_bias, dn_out_norm, w_out, norm2_w, peer_w_query, peer_sub_keys1, peer_sub_keys2, peer_expert_down, peer_expert_up):
    b, t, d = x.shape
    x2 = x.reshape(b * t, d)
    for l in range(norm1_w.shape[0]):
        a_q, a_k, a_v, d_qkv, d_gate, d_ba = in_proj(x2, norm1_w[l], w_in[l])
        seq = lambda a: a.reshape(b, t, a.shape[-1])
        att = attention(seq(a_q), seq(a_k), seq(a_v), att_q_norm[l], att_k_norm[l], att_sinks[l])
        dn = delta_net(seq(d_qkv), seq(d_gate), seq(d_ba), dn_conv_w[l], dn_A_log[l], dn_dt_bias[l],
                       dn_out_norm[l])
        x1, h2, qry = mix_proj(att.reshape(b * t, -1), dn.reshape(b * t, -1), x2, w_out[l], norm2_w[l],
                               peer_w_query[l])
        theta, e1, s2, e2 = peer_route(qry, peer_sub_keys1[l], peer_sub_keys2[l])
        x2 = peer_experts(h2, x1, peer_expert_down[l], peer_expert_up[l], theta, e1, s2, e2)
    return x2.reshape(b, t, d)
```

```python
import jax
import jax.numpy as jnp
from jax import lax
from jax.experimental import pallas as pl
from jax.experimental.pallas import tpu as pltpu

EPS = 1e-6
LANES = 128
SUBLANES = 8
VMEM_LIMIT = 56 * 1024 * 1024

ATT_HEADS, ATT_KV_HEADS, ATT_HEAD_DIM = 8, 2, 64
ATT_GROUP = ATT_HEADS // ATT_KV_HEADS
ATT_BLOCK = 128
ATT_STEP_BLOCKS = 4
ROPE_THETA = 500000.0
ROPE_DIM = ATT_HEAD_DIM // 4
DN_HEADS, DN_DIM, DN_CONV, DN_CHUNK = 4, 128, 4, 64
ATT_Q_W = ATT_HEADS * ATT_HEAD_DIM
ATT_KV_W = ATT_KV_HEADS * ATT_HEAD_DIM
DN_W = DN_HEADS * DN_DIM
DN_QKV_W = 3 * DN_W
IN_MAIN_W = ATT_Q_W + 2 * ATT_KV_W + DN_QKV_W + DN_W
PEER_HEADS, PEER_KEYS, PEER_TOPK = 8, 128, 16
PEER_HALF = 128

F32 = jnp.float32
BF16 = jnp.bfloat16
NT_DIMS = (((1,), (1,)), ((), ()))
TN_DIMS = (((0,), (0,)), ((), ()))


def _params(*sem):
    return pltpu.CompilerParams(dimension_semantics=sem, vmem_limit_bytes=VMEM_LIMIT)


def _bdot(a, b):
    return jnp.dot(a.astype(BF16), b.astype(BF16), preferred_element_type=F32)


def _hi_lo(x):
    hi = x.astype(BF16)
    return hi, (x - hi.astype(F32)).astype(BF16)


def _dot_hi(a, b):
    ah, al = _hi_lo(a)
    bh, bl = _hi_lo(b)
    return (jnp.dot(ah, bh, preferred_element_type=F32) + jnp.dot(ah, bl, preferred_element_type=F32)
            + jnp.dot(al, bh, preferred_element_type=F32))


def _dot_exact_lhs(lhs, x):
    hi = x.astype(BF16)
    r = x - hi.astype(F32)
    mid = r.astype(BF16)
    lo = (r - mid.astype(F32)).astype(BF16)
    return (jnp.dot(lhs, hi, preferred_element_type=F32) + jnp.dot(lhs, mid, preferred_element_type=F32)
            + jnp.dot(lhs, lo, preferred_element_type=F32))


def _in_proj_kernel(x_ref, nw_ref, w_ref, q_ref, k_ref, v_ref, dqkv_ref, gate_ref, ba_ref):
    x = x_ref[...]
    h = x * lax.rsqrt(jnp.mean(x * x, axis=-1, keepdims=True) + EPS) * nw_ref[...]
    p = jnp.dot(h.astype(BF16), w_ref[...], preferred_element_type=F32)
    o = 0
    for ref in (q_ref, k_ref, v_ref, dqkv_ref, gate_ref, ba_ref):
        w = ref.shape[-1]
        ref[...] = p[:, o:o + w]
        o += w


def in_proj(x2, norm_w, w_in, tm=512):
    n, d = x2.shape
    small = w_in.shape[1] - IN_MAIN_W
    w_pad = jnp.pad(w_in, ((0, 0), (0, LANES - small))).astype(BF16)
    widths = (ATT_Q_W, ATT_KV_W, ATT_KV_W, DN_QKV_W, DN_W, LANES)
    return pl.pallas_call(
        _in_proj_kernel,
        grid=(n // tm,),
        in_specs=[pl.BlockSpec((tm, d), lambda i: (i, 0)),
                  pl.BlockSpec((1, d), lambda i: (0, 0)),
                  pl.BlockSpec(w_pad.shape, lambda i: (0, 0))],
        out_specs=[pl.BlockSpec((tm, w), lambda i: (i, 0)) for w in widths],
        out_shape=[jax.ShapeDtypeStruct((n, w), F32) for w in widths],
        compiler_params=_params("parallel"),
        name="in_proj",
    )(x2, norm_w.reshape(1, d), w_pad)


def _norm_rope(x, w, c, sm, sp):
    rows, width = x.shape
    x2 = x * x
    parts = []
    for h in range(width // ATT_HEAD_DIM):
        ms = jnp.mean(x2[:, h * ATT_HEAD_DIM:(h + 1) * ATT_HEAD_DIM], axis=-1, keepdims=True)
        parts.append(jnp.broadcast_to(lax.rsqrt(ms + EPS), (rows, ATT_HEAD_DIM)))
    xn = x * jnp.concatenate(parts, axis=1) * w
    half = ROPE_DIM // 2
    return xn * c + pltpu.roll(xn, width - half, 1) * sm + pltpu.roll(xn, half, 1) * sp


def _attn_kernel(q_ref, kc_ref, kp_ref, vc_ref, vp_ref, c_ref, sm_ref, sp_ref,
                 cp_ref, smp_ref, spp_ref, qw_ref, kw_ref, sink_ref, o_ref):
    i = pl.program_id(1)
    blk = ATT_BLOCK
    q_all = _norm_rope(q_ref[0], qw_ref[...], c_ref[...], sm_ref[...], sp_ref[...])
    kw = kw_ref[...]
    kc_all = _norm_rope(kc_ref[0], kw, c_ref[:, :ATT_KV_W], sm_ref[:, :ATT_KV_W], sp_ref[:, :ATT_KV_W])
    kp_first = _norm_rope(kp_ref[0], kw, cp_ref[...], smp_ref[...], spp_ref[...])
    vc_all = vc_ref[0]
    rows = ATT_GROUP * blk
    qi = lax.broadcasted_iota(jnp.int32, (rows, blk), 0) % blk
    kj = lax.broadcasted_iota(jnp.int32, (rows, blk), 1)
    mask_c = kj <= qi
    scale = ATT_HEAD_DIM ** -0.5
    for s in range(ATT_STEP_BLOCKS):
        cur = slice(s * blk, (s + 1) * blk)
        before = slice((s - 1) * blk, s * blk)
        q, kc, vc = q_all[cur], kc_all[cur], vc_all[cur]
        kp = kp_first if s == 0 else kc_all[before]
        vp = vp_ref[0] if s == 0 else vc_all[before]
        mask_p = jnp.logical_and(kj > qi, i > 0) if s == 0 else kj > qi
        outs = []
        for g in range(ATT_KV_HEADS):
            heads = range(g * ATT_GROUP, (g + 1) * ATT_GROUP)
            qg = jnp.concatenate([q[:, h * ATT_HEAD_DIM:(h + 1) * ATT_HEAD_DIM] for h in heads], axis=0)
            sink = jnp.concatenate([jnp.broadcast_to(sink_ref[h:h + 1, 0:1], (blk, 1)) for h in heads], axis=0)
            lo, hi = g * ATT_HEAD_DIM, (g + 1) * ATT_HEAD_DIM
            qb = qg.astype(BF16)
            s_c = lax.dot_general(qb, kc[:, lo:hi].astype(BF16), NT_DIMS, preferred_element_type=F32) * scale
            s_p = lax.dot_general(qb, kp[:, lo:hi].astype(BF16), NT_DIMS, preferred_element_type=F32) * scale
            s_c = jnp.where(mask_c, s_c, -jnp.inf)
            s_p = jnp.where(mask_p, s_p, -jnp.inf)
            m = jnp.maximum(jnp.maximum(jnp.max(s_c, axis=-1, keepdims=True),
                                        jnp.max(s_p, axis=-1, keepdims=True)), sink)
            p_c = jnp.exp(s_c - m)
            p_p = jnp.exp(s_p - m)
            denom = (jnp.sum(p_c, axis=-1, keepdims=True) + jnp.sum(p_p, axis=-1, keepdims=True)
                     + jnp.exp(sink - m))
            og = (_bdot(p_c, vc[:, lo:hi]) + _bdot(p_p, vp[:, lo:hi])) / denom
            outs += [og[j * blk:(j + 1) * blk] for j in range(ATT_GROUP)]
        o_ref[0, cur] = jnp.concatenate(outs, axis=1).astype(o_ref.dtype)


def _rope_tables(t):
    half = ROPE_DIM // 2
    inv_freq = ROPE_THETA ** (-(jnp.arange(half, dtype=F32) * 2.0 / ROPE_DIM))
    ang = jnp.arange(t).astype(F32)[:, None] * inv_freq[None, :]
    cos, sin = jnp.cos(ang), jnp.sin(ang)
    rest = ATT_HEAD_DIM - ROPE_DIM
    c = jnp.concatenate([cos, cos, jnp.ones((t, rest), F32)], axis=1)
    sm = jnp.concatenate([-sin, jnp.zeros((t, half + rest), F32)], axis=1)
    sp = jnp.concatenate([jnp.zeros((t, half), F32), sin, jnp.zeros((t, rest), F32)], axis=1)
    return tuple(jnp.tile(a, (1, ATT_HEADS)) for a in (c, sm, sp))


def attention(q, k, v, q_norm, k_norm, sinks):
    b, t, _ = q.shape
    blk = ATT_BLOCK
    c, sm, sp = _rope_tables(t)
    nb = ATT_STEP_BLOCKS
    step = nb * blk
    cur = lambda bi, i: (bi, i, 0)
    prev = lambda bi, i: (bi, jnp.maximum(i * nb - 1, 0), 0)
    tcur = lambda bi, i: (i, 0)
    tprev = lambda bi, i: (jnp.maximum(i * nb - 1, 0), 0)
    const = lambda bi, i: (0, 0)
    return pl.pallas_call(
        _attn_kernel,
        grid=(b, t // step),
        in_specs=[pl.BlockSpec((1, step, ATT_Q_W), cur),
                  pl.BlockSpec((1, step, ATT_KV_W), cur), pl.BlockSpec((1, blk, ATT_KV_W), prev),
                  pl.BlockSpec((1, step, ATT_KV_W), cur), pl.BlockSpec((1, blk, ATT_KV_W), prev),
                  pl.BlockSpec((step, ATT_Q_W), tcur), pl.BlockSpec((step, ATT_Q_W), tcur),
                  pl.BlockSpec((step, ATT_Q_W), tcur),
                  pl.BlockSpec((blk, ATT_KV_W), tprev), pl.BlockSpec((blk, ATT_KV_W), tprev),
                  pl.BlockSpec((blk, ATT_KV_W), tprev),
                  pl.BlockSpec((1, ATT_Q_W), const), pl.BlockSpec((1, ATT_KV_W), const),
                  pl.BlockSpec((ATT_HEADS, LANES), const)],
        out_specs=pl.BlockSpec((1, step, ATT_Q_W), cur),
        out_shape=jax.ShapeDtypeStruct((b, t, ATT_Q_W), BF16),
        compiler_params=_params("parallel", "parallel"),
        name="swa_attention",
    )(q, k, k, v, v, c, sm, sp, c, sm, sp,
      jnp.tile(q_norm, ATT_HEADS).reshape(1, ATT_Q_W), jnp.tile(k_norm, ATT_KV_HEADS).reshape(1, ATT_KV_W),
      jnp.broadcast_to(sinks[:, None], (ATT_HEADS, LANES)))


def _softplus(x):
    return jnp.maximum(x, 0.0) + jnp.log1p(jnp.exp(-jnp.abs(x)))


def _dn_kernel(x_ref, gate_ref, ba_ref, cw_ref, alog_ref, dtb_ref, onw_ref, o_ref, *scratch):
    ck = DN_CHUNK
    nb = x_ref.shape[0]
    halo_refs = scratch[:nb]
    s_refs = scratch[nb:]

    @pl.when(pl.program_id(0) == 0)
    def _():
        for ref in scratch:
            ref[...] = jnp.zeros_like(ref)

    cw = cw_ref[...]
    onw = onw_ref[...]
    neg_a = -jnp.exp(alog_ref[...])
    dtb = dtb_ref[...]
    ri = lax.broadcasted_iota(jnp.int32, (ck, ck), 0)
    ci = lax.broadcasted_iota(jnp.int32, (ck, ck), 1)
    tril = ri >= ci
    strict = ri > ci
    tril_b = jnp.where(tril, 1.0, 0.0).astype(BF16)
    eye_f = jnp.where(ri == ci, 1.0, 0.0).astype(F32)

    chains = [(b, h) for b in range(nb) for h in range(DN_HEADS)]
    ys, betas, gs_all = [], [], []
    for b in range(nb):
        x = x_ref[b]
        xe = jnp.concatenate([halo_refs[b][...], x], axis=0)
        halo_refs[b][...] = x[ck - SUBLANES:ck]
        y = x * cw[DN_CONV - 1:DN_CONV]
        for s in range(1, DN_CONV):
            y = y + xe[SUBLANES - s:SUBLANES - s + ck] * cw[DN_CONV - 1 - s:DN_CONV - s]
        ys.append(y * jax.nn.sigmoid(y))
        ba = ba_ref[b]
        betas.append(jax.nn.sigmoid(ba))
        gs_all.append(neg_a * _softplus(ba + dtb))

    qs, ks, kbs, vbs, gbs = [], [], [], [], []
    for b, h in chains:
        y = ys[b]
        q = y[:, h * DN_DIM:(h + 1) * DN_DIM]
        k = y[:, DN_W + h * DN_DIM:DN_W + (h + 1) * DN_DIM]
        v = y[:, 2 * DN_W + h * DN_DIM:2 * DN_W + (h + 1) * DN_DIM]
        q = q * lax.rsqrt(jnp.sum(q * q, axis=-1, keepdims=True) + EPS) * (DN_DIM ** -0.5)
        k = k * lax.rsqrt(jnp.sum(k * k, axis=-1, keepdims=True) + EPS)
        beta = betas[b][:, h:h + 1]
        qs.append(q)
        ks.append(k)
        kbs.append(k * beta)
        vbs.append(v * beta)
        gbs.append(jnp.broadcast_to(gs_all[b][:, DN_HEADS + h:DN_HEADS + h + 1], (ck, ck)))
    nc = len(chains)
    gc_i = [_dot_exact_lhs(tril_b, gb) for gb in gbs]
    kq = [lax.dot_general(jnp.concatenate([kbs[c], qs[c]], axis=0).astype(BF16), ks[c].astype(BF16), NT_DIMS,
                          preferred_element_type=F32) for c in range(nc)]
    gcols, glasts, intras, t_mats, pws = [], [], [], [], []
    for c in range(nc):
        gc_j = gc_i[c].T
        decay = jnp.where(tril, jnp.exp(jnp.where(tril, gc_i[c] - gc_j, 0.0)), 0.0)
        gcols.append(gc_i[c][:, 0:1])
        glasts.append(gc_j[:, ck - 1:ck])
        a = -jnp.where(strict, kq[c][:ck] * decay, 0.0)
        intras.append(jnp.where(tril, kq[c][ck:] * decay, 0.0))
        t_mats.append(eye_f + a)
        pws.append(a)
    pws = [_dot_hi(p, p) for p in pws]
    for _ in range(4):
        both = [_dot_hi(pws[c], jnp.concatenate([t_mats[c], pws[c]], axis=1)) for c in range(nc)]
        t_mats = [t_mats[c] + both[c][:, :ck] for c in range(nc)]
        pws = [both[c][:, ck:] for c in range(nc)]
    t_mats = [t_mats[c] + _dot_hi(pws[c], t_mats[c]) for c in range(nc)]
    egcs = [jnp.exp(g) for g in gcols]
    uw = [_bdot(t_mats[c], jnp.concatenate([vbs[c], kbs[c] * egcs[c]], axis=1)) for c in range(nc)]
    sts = [s_refs[c][...] for c in range(nc)]
    ws = [_bdot(jnp.concatenate([uw[c][:, DN_DIM:], qs[c] * egcs[c]], axis=0), sts[c]) for c in range(nc)]
    v_new = [uw[c][:, :DN_DIM] - ws[c][:ck] for c in range(nc)]
    o_intra = [_bdot(intras[c], v_new[c]) for c in range(nc)]
    s_add = [lax.dot_general((ks[c] * jnp.exp(glasts[c] - gcols[c])).astype(BF16), v_new[c].astype(BF16), TN_DIMS,
                             preferred_element_type=F32) for c in range(nc)]
    for c, (b, h) in enumerate(chains):
        egl = jnp.exp(jnp.concatenate([glasts[c], glasts[c]], axis=0))
        s_refs[c][...] = sts[c] * egl + s_add[c]
        o = ws[c][ck:] + o_intra[c]
        on = o * lax.rsqrt(jnp.mean(o * o, axis=-1, keepdims=True) + EPS) * onw
        gt = gate_ref[b, :, h * DN_DIM:(h + 1) * DN_DIM]
        o_ref[b, :, h * DN_DIM:(h + 1) * DN_DIM] = (on * (gt * jax.nn.sigmoid(gt))).astype(o_ref.dtype)


def _lane_row(vals, offset):
    return jnp.zeros((1, LANES), F32).at[0, offset:offset + vals.shape[0]].set(vals)


def delta_net(dqkv, gate, ba, conv_w, a_log, dt_bias, out_norm):
    b, t, _ = dqkv.shape
    ck = DN_CHUNK
    blk = lambda c: (0, c, 0)
    const = lambda c: (0, 0)
    return pl.pallas_call(
        _dn_kernel,
        grid=(t // ck,),
        in_specs=[pl.BlockSpec((b, ck, DN_QKV_W), blk), pl.BlockSpec((b, ck, DN_W), blk),
                  pl.BlockSpec((b, ck, LANES), blk),
                  pl.BlockSpec((DN_CONV, DN_QKV_W), const),
                  pl.BlockSpec((1, LANES), const), pl.BlockSpec((1, LANES), const),
                  pl.BlockSpec((1, DN_DIM), const)],
        out_specs=pl.BlockSpec((b, ck, DN_W), blk),
        out_shape=jax.ShapeDtypeStruct((b, t, DN_W), BF16),
        scratch_shapes=([pltpu.VMEM((SUBLANES, DN_QKV_W), F32)] * b
                        + [pltpu.VMEM((DN_DIM, DN_DIM), F32)] * (b * DN_HEADS)),
        compiler_params=_params("arbitrary"),
        name="gated_delta_net",
    )(dqkv, gate, ba, conv_w, _lane_row(a_log, DN_HEADS), _lane_row(dt_bias, DN_HEADS),
      out_norm.reshape(1, DN_DIM))


def _mix_kernel(att_ref, dn_ref, x_ref, wa_ref, wd_ref, n2_ref, wq_ref, x1_ref, h2_ref, qry_ref):
    x1 = (x_ref[...] + jnp.dot(att_ref[...], wa_ref[...], preferred_element_type=F32)
          + jnp.dot(dn_ref[...], wd_ref[...], preferred_element_type=F32))
    x1_ref[...] = x1
    h2 = (x1 * lax.rsqrt(jnp.mean(x1 * x1, axis=-1, keepdims=True) + EPS) * n2_ref[...]).astype(BF16)
    h2_ref[...] = h2
    qry_ref[...] = jnp.dot(h2, wq_ref[...], preferred_element_type=F32)


def mix_proj(att, dn, x2, w_out, norm2_w, w_query, tm=512):
    n, d = x2.shape
    wa = w_out[:ATT_Q_W].astype(BF16)
    wd = w_out[ATT_Q_W:].astype(BF16)
    wq = w_query.astype(BF16)
    qw = wq.shape[1]
    row = lambda i: (i, 0)
    const = lambda i: (0, 0)
    return pl.pallas_call(
        _mix_kernel,
        grid=(n // tm,),
        in_specs=[pl.BlockSpec((tm, ATT_Q_W), row), pl.BlockSpec((tm, DN_W), row), pl.BlockSpec((tm, d), row),
                  pl.BlockSpec(wa.shape, const), pl.BlockSpec(wd.shape, const),
                  pl.BlockSpec((1, d), const), pl.BlockSpec(wq.shape, const)],
        out_specs=[pl.BlockSpec((tm, d), row), pl.BlockSpec((tm, d), row), pl.BlockSpec((tm, qw), row)],
        out_shape=[jax.ShapeDtypeStruct((n, d), F32), jax.ShapeDtypeStruct((n, d), BF16),
                   jax.ShapeDtypeStruct((n, qw), F32)],
        compiler_params=_params("parallel"),
        name="mix_proj",
    )(att, dn, x2, wa, wd, norm2_w.reshape(1, d), wq)


def _merge_exchange_pairs(n):
    pairs = []
    t = max(1, (n - 1).bit_length())
    p = 1 << (t - 1)
    while p > 0:
        q, r, d = 1 << (t - 1), 0, p
        while d > 0:
            pairs += [(i, i + d) for i in range(n - d) if (i & p) == r]
            d, q, r = q - p, q >> 1, p
        p >>= 1
    return pairs


def _sort_desc(xs):
    xs = list(xs)
    for a, b in _merge_exchange_pairs(len(xs)):
        xs[a], xs[b] = jnp.maximum(xs[a], xs[b]), jnp.minimum(xs[a], xs[b])
    return xs


def _bitonic_merge_desc(xs):
    xs = list(xs)
    n = len(xs)
    d = n // 2
    while d > 0:
        for i in range(n):
            if (i & d) == 0:
                xs[i], xs[i + d] = jnp.maximum(xs[i], xs[i + d]), jnp.minimum(xs[i], xs[i + d])
        d //= 2
    return xs


def _top_of_union(xs, shift):
    k = len(xs)
    rolled = [None if x is None else pltpu.roll(x, shift, 0) for x in xs]
    out = []
    for i in range(k):
        a, b = xs[i], rolled[k - 1 - i]
        out.append(b if a is None else a if b is None else jnp.maximum(a, b))
    return out


def _top_sorted_all_sublanes(slabs):
    xs = _sort_desc(slabs)[:PEER_TOPK]
    xs += [None] * (PEER_TOPK - len(xs))
    for shift in (4, 2):
        xs = _bitonic_merge_desc(_top_of_union(xs, shift))
    return _top_of_union(xs, 1)


def _pack_sublanes(vals, sub):
    out = vals[0]
    for s in range(1, SUBLANES):
        out = jnp.where(sub == s, vals[s], out)
    return out


def _route_kernel(q_ref, k1_ref, k2_ref, th_ref, e1_ref, s2_ref, e2_ref):
    q = q_ref[...].astype(BF16)
    s1 = lax.dot_general(k1_ref[...], q[:, :PEER_HALF], NT_DIMS, preferred_element_type=F32)
    s2 = lax.dot_general(k2_ref[...], q[:, PEER_HALF:], NT_DIMS, preferred_element_type=F32)
    nslab = PEER_KEYS // SUBLANES
    s1s = [s1[SUBLANES * v:SUBLANES * (v + 1)] for v in range(nslab)]
    s2s = [s2[SUBLANES * v:SUBLANES * (v + 1)] for v in range(nslab)]
    v1 = _bitonic_merge_desc(_top_sorted_all_sublanes(s1s))
    v2 = _bitonic_merge_desc(_top_sorted_all_sublanes(s2s))
    sub = lax.broadcasted_iota(jnp.int32, v1[0].shape, 0)
    v2a, v2b = _pack_sublanes(v2[:SUBLANES], sub), _pack_sublanes(v2[SUBLANES:], sub)
    v1b = _pack_sublanes(v1[SUBLANES:], sub)
    cands = [v1[0] + v2a, v1[0] + v2b] + [v1[i] + v2a for i in range(1, SUBLANES)] + [v1b + v2[0]]
    top16 = _top_sorted_all_sublanes(cands)
    tau = top16[0]
    for x in top16[1:]:
        tau = jnp.minimum(tau, x)
    top = v1[0] + v2[0]
    z = None
    for c in cands:
        term = jnp.where(c >= tau, jnp.exp(c - top), 0.0)
        z = term if z is None else z + term
    for shift in (4, 2, 1):
        z = z + pltpu.roll(z, shift, 0)
    inv_z = 0.5 / z
    dense_r = 4
    th_rank = []
    for i, jmax in ((0, 16), (1, 8), (2, 5)):
        t = jnp.full_like(tau, jnp.inf)
        for r in range(dense_r, jmax):
            t = jnp.where(v1[i] + v2[r] >= tau, v2[r], t)
        th_rank.append(t)
    for v in range(nslab):
        rows = slice(SUBLANES * v, SUBLANES * (v + 1))
        t = jnp.full_like(tau, jnp.inf)
        for r in range(dense_r):
            t = jnp.where(s1s[v] + v2[r] >= tau, v2[r], t)
        for i in range(len(th_rank)):
            t = jnp.where(s1s[v] == v1[i], jnp.minimum(t, th_rank[i]), t)
        th_ref[0, rows] = t
        e1_ref[0, rows] = jnp.exp(s1s[v] - v1[0]) * inv_z
        s2_ref[0, rows] = s2s[v]
        e2_ref[0, rows] = jnp.exp(s2s[v] - v2[0])


def peer_route(qry, keys1, keys2, tn=256):
    n = qry.shape[0]
    qblk = 2 * PEER_HALF
    out = pl.BlockSpec((1, PEER_KEYS, tn), lambda i, h: (h, 0, i))
    const = lambda i, h: (0, 0)
    return pl.pallas_call(
        _route_kernel,
        grid=(n // tn, PEER_HEADS),
        in_specs=[pl.BlockSpec((tn, qblk), lambda i, h: (i, h)),
                  pl.BlockSpec((PEER_KEYS, PEER_HALF), const), pl.BlockSpec((PEER_KEYS, PEER_HALF), const)],
        out_specs=[out] * 4,
        out_shape=[jax.ShapeDtypeStruct((PEER_HEADS, PEER_KEYS, n), F32)] * 4,
        compiler_params=_params("parallel", "parallel"),
        name="peer_route",
    )(qry, keys1.astype(BF16), keys2.astype(BF16))


PEER_EXPERT_BLOCK = 2048
PEER_TOKEN_TILE = 512
GATE_ROWS = 32
GATE_KEYS = 4


def _expert_kernel(h_ref, x1_ref, wd_ref, wu_ref, th_ref, e1_ref, s2_ref, e2_ref, o_ref, act_ref, hw_ref, acc_ref):
    e = pl.program_id(1)
    tn = h_ref.shape[0]

    @pl.when(e == 0)
    def _():
        acc_ref[...] = jnp.zeros_like(acc_ref)

    act_ref[...] = lax.dot_general(wd_ref[...], h_ref[...], NT_DIMS, preferred_element_type=F32)
    n_jb = PEER_KEYS // GATE_ROWS
    n_ab = PEER_EXPERT_BLOCK // PEER_KEYS // GATE_KEYS

    def gate_tile(t, carry):
        j0 = pl.multiple_of((t % n_jb) * GATE_ROWS, GATE_ROWS)
        ls = pl.ds(pl.multiple_of((t // n_jb) * LANES, LANES), LANES)
        for ab in range(n_ab):
            keys = range(ab * GATE_KEYS, (ab + 1) * GATE_KEYS)
            gs = [jnp.zeros((GATE_ROWS, LANES), F32) for _ in keys]
            for hd in range(PEER_HEADS):
                s2 = s2_ref[hd, pl.ds(j0, GATE_ROWS), ls]
                e2 = e2_ref[hd, pl.ds(j0, GATE_ROWS), ls]
                for n, al in enumerate(keys):
                    sel = s2 >= th_ref[hd, al:al + 1, ls]
                    gs[n] = gs[n] + jnp.where(sel, e2 * e1_ref[hd, al:al + 1, ls], 0.0)
            for n, al in enumerate(keys):
                r0 = pl.multiple_of(al * PEER_KEYS + j0, GATE_ROWS)
                z = act_ref[pl.ds(r0, GATE_ROWS), ls]
                gelu2 = z * (1.0 + lax.erf(z * (2.0 ** -0.5)))
                hw_ref[pl.ds(r0, GATE_ROWS), ls] = (gs[n] * gelu2).astype(BF16)
        return carry

    lax.fori_loop(0, n_jb * (tn // LANES), gate_tile, 0)
    acc_ref[...] += jnp.dot(wu_ref[...], hw_ref[...], preferred_element_type=F32)

    @pl.when(e == pl.num_programs(1) - 1)
    def _():
        o_ref[...] = x1_ref[...] + acc_ref[...].T


def peer_experts(h2, x1, expert_down, expert_up, theta, e1, s2, e2):
    n, d = h2.shape
    ne = expert_down.shape[0]
    tn, eb = PEER_TOKEN_TILE, PEER_EXPERT_BLOCK
    ab = eb // PEER_KEYS
    wd = expert_down.astype(BF16)
    wu_t = expert_up.astype(BF16).T
    tok = lambda i, e: (i, 0)
    full = pl.BlockSpec((PEER_HEADS, PEER_KEYS, tn), lambda i, e: (0, 0, i))
    part = pl.BlockSpec((PEER_HEADS, ab, tn), lambda i, e: (0, e, i))
    return pl.pallas_call(
        _expert_kernel,
        grid=(n // tn, ne // eb),
        in_specs=[pl.BlockSpec((tn, d), tok), pl.BlockSpec((tn, d), tok),
                  pl.BlockSpec((eb, d), lambda i, e: (e, 0)), pl.BlockSpec((d, eb), lambda i, e: (0, e)),
                  part, part, full, full],
        out_specs=pl.BlockSpec((tn, d), tok),
        out_shape=jax.ShapeDtypeStruct((n, d), F32),
        scratch_shapes=[pltpu.VMEM((eb, tn), F32), pltpu.VMEM((eb, tn), BF16), pltpu.VMEM((d, tn), F32)],
        compiler_params=_params("parallel", "arbitrary"),
        name="peer_experts",
    )(h2, x1, wd, wu_t, theta, e1, s2, e2)


def kernel(x, norm1_w, w_in, att_q_norm, att_k_norm, att_sinks, dn_conv_w, dn_A_log, dn_dt_bias, dn_out_norm, w_out, norm2_w, peer_w_query, peer_sub_keys1, peer_sub_keys2, peer_expert_down, peer_expert_up):
    b, t, d = x.shape
    x2 = x.reshape(b * t, d)
    for l in range(norm1_w.shape[0]):
        a_q, a_k, a_v, d_qkv, d_gate, d_ba = in_proj(x2, norm1_w[l], w_in[l])
        seq = lambda a: a.reshape(b, t, a.shape[-1])
        att = attention(seq(a_q), seq(a_k), seq(a_v), att_q_norm[l], att_k_norm[l], att_sinks[l])
        dn = delta_net(seq(d_qkv), seq(d_gate), seq(d_ba), dn_conv_w[l], dn_A_log[l], dn_dt_bias[l],
                       dn_out_norm[l])
        x1, h2, qry = mix_proj(att.reshape(b * t, -1), dn.reshape(b * t, -1), x2, w_out[l], norm2_w[l],
                               peer_w_query[l])
        theta, e1, s2, e2 = peer_route(qry, peer_sub_keys1[l], peer_sub_keys2[l])
        x2 = peer_experts(h2, x1, peer_expert_down[l], peer_expert_up[l], theta, e1, s2, e2)
    return x2.reshape(b, t, d)
```

```python
import jax
import jax.numpy as jnp
from jax import lax
from jax.experimental import pallas as pl
from jax.experimental.pallas import tpu as pltpu

EPS = 1e-6
LANES = 128
SUBLANES = 8
VMEM_LIMIT = 56 * 1024 * 1024

ATT_HEADS, ATT_KV_HEADS, ATT_HEAD_DIM = 8, 2, 64
ATT_GROUP = ATT_HEADS // ATT_KV_HEADS
ATT_BLOCK = 128
ATT_STEP_BLOCKS = 8
ROPE_THETA = 500000.0
ROPE_DIM = ATT_HEAD_DIM // 4
DN_HEADS, DN_DIM, DN_CONV, DN_CHUNK = 4, 128, 4, 64
ATT_Q_W = ATT_HEADS * ATT_HEAD_DIM
ATT_KV_W = ATT_KV_HEADS * ATT_HEAD_DIM
DN_W = DN_HEADS * DN_DIM
DN_QKV_W = 3 * DN_W
IN_MAIN_W = ATT_Q_W + 2 * ATT_KV_W + DN_QKV_W + DN_W
PEER_HEADS, PEER_KEYS, PEER_TOPK = 8, 128, 16
PEER_HALF = 128

F32 = jnp.float32
BF16 = jnp.bfloat16
NT_DIMS = (((1,), (1,)), ((), ()))
TN_DIMS = (((0,), (0,)), ((), ()))


def _params(*sem):
    return pltpu.CompilerParams(dimension_semantics=sem, vmem_limit_bytes=VMEM_LIMIT)


def _bdot(a, b):
    return jnp.dot(a.astype(BF16), b.astype(BF16), preferred_element_type=F32)


def _hi_lo(x):
    hi = x.astype(BF16)
    return hi, (x - hi.astype(F32)).astype(BF16)


def _dot_hi(a, b):
    ah, al = _hi_lo(a)
    bh, bl = _hi_lo(b)
    return (jnp.dot(ah, bh, preferred_element_type=F32) + jnp.dot(ah, bl, preferred_element_type=F32)
            + jnp.dot(al, bh, preferred_element_type=F32))


def _dot_exact_lhs(lhs, x):
    hi = x.astype(BF16)
    r = x - hi.astype(F32)
    mid = r.astype(BF16)
    lo = (r - mid.astype(F32)).astype(BF16)
    return (jnp.dot(lhs, hi, preferred_element_type=F32) + jnp.dot(lhs, mid, preferred_element_type=F32)
            + jnp.dot(lhs, lo, preferred_element_type=F32))


def _in_proj_kernel(x_ref, nw_ref, w_ref, q_ref, k_ref, v_ref, dqkv_ref, gate_ref, ba_ref):
    x = x_ref[...]
    h = x * lax.rsqrt(jnp.mean(x * x, axis=-1, keepdims=True) + EPS) * nw_ref[...]
    p = jnp.dot(h.astype(BF16), w_ref[...], preferred_element_type=F32)
    o = 0
    for ref in (q_ref, k_ref, v_ref, dqkv_ref, gate_ref, ba_ref):
        w = ref.shape[-1]
        ref[...] = p[:, o:o + w]
        o += w


def in_proj(x2, norm_w, w_in, tm=512):
    n, d = x2.shape
    small = w_in.shape[1] - IN_MAIN_W
    w_pad = jnp.pad(w_in, ((0, 0), (0, LANES - small))).astype(BF16)
    widths = (ATT_Q_W, ATT_KV_W, ATT_KV_W, DN_QKV_W, DN_W, LANES)
    return pl.pallas_call(
        _in_proj_kernel,
        grid=(n // tm,),
        in_specs=[pl.BlockSpec((tm, d), lambda i: (i, 0)),
                  pl.BlockSpec((1, d), lambda i: (0, 0)),
                  pl.BlockSpec(w_pad.shape, lambda i: (0, 0))],
        out_specs=[pl.BlockSpec((tm, w), lambda i: (i, 0)) for w in widths],
        out_shape=[jax.ShapeDtypeStruct((n, w), F32) for w in widths],
        compiler_params=_params("parallel"),
        name="in_proj",
    )(x2, norm_w.reshape(1, d), w_pad)


def _norm_rope(x, w, c, sm, sp):
    rows, width = x.shape
    x2 = x * x
    parts = []
    for h in range(width // ATT_HEAD_DIM):
        ms = jnp.mean(x2[:, h * ATT_HEAD_DIM:(h + 1) * ATT_HEAD_DIM], axis=-1, keepdims=True)
        parts.append(jnp.broadcast_to(lax.rsqrt(ms + EPS), (rows, ATT_HEAD_DIM)))
    xn = x * jnp.concatenate(parts, axis=1) * w
    half = ROPE_DIM // 2
    return xn * c + pltpu.roll(xn, width - half, 1) * sm + pltpu.roll(xn, half, 1) * sp


def _attn_kernel(q_ref, kc_ref, kp_ref, vc_ref, vp_ref, c_ref, sm_ref, sp_ref,
                 cp_ref, smp_ref, spp_ref, qw_ref, kw_ref, sink_ref, o_ref):
    i = pl.program_id(1)
    blk = ATT_BLOCK
    q_all = _norm_rope(q_ref[0], qw_ref[...], c_ref[...], sm_ref[...], sp_ref[...])
    kw = kw_ref[...]
    kc_all = _norm_rope(kc_ref[0], kw, c_ref[:, :ATT_KV_W], sm_ref[:, :ATT_KV_W], sp_ref[:, :ATT_KV_W])
    kp_first = _norm_rope(kp_ref[0], kw, cp_ref[...], smp_ref[...], spp_ref[...])
    vc_all = vc_ref[0]
    rows = ATT_GROUP * blk
    qi = lax.broadcasted_iota(jnp.int32, (rows, blk), 0) % blk
    kj = lax.broadcasted_iota(jnp.int32, (rows, blk), 1)
    mask_c = kj <= qi
    scale = ATT_HEAD_DIM ** -0.5
    for s in range(ATT_STEP_BLOCKS):
        cur = slice(s * blk, (s + 1) * blk)
        before = slice((s - 1) * blk, s * blk)
        q, kc, vc = q_all[cur], kc_all[cur], vc_all[cur]
        kp = kp_first if s == 0 else kc_all[before]
        vp = vp_ref[0] if s == 0 else vc_all[before]
        mask_p = jnp.logical_and(kj > qi, i > 0) if s == 0 else kj > qi
        outs = []
        for g in range(ATT_KV_HEADS):
            heads = range(g * ATT_GROUP, (g + 1) * ATT_GROUP)
            qg = jnp.concatenate([q[:, h * ATT_HEAD_DIM:(h + 1) * ATT_HEAD_DIM] for h in heads], axis=0)
            sink = jnp.concatenate([jnp.broadcast_to(sink_ref[h:h + 1, 0:1], (blk, 1)) for h in heads], axis=0)
            lo, hi = g * ATT_HEAD_DIM, (g + 1) * ATT_HEAD_DIM
            qb = qg.astype(BF16)
            s_c = lax.dot_general(qb, kc[:, lo:hi].astype(BF16), NT_DIMS, preferred_element_type=F32) * scale
            s_p = lax.dot_general(qb, kp[:, lo:hi].astype(BF16), NT_DIMS, preferred_element_type=F32) * scale
            s_c = jnp.where(mask_c, s_c, -jnp.inf)
            s_p = jnp.where(mask_p, s_p, -jnp.inf)
            m = jnp.maximum(jnp.maximum(jnp.max(s_c, axis=-1, keepdims=True),
                                        jnp.max(s_p, axis=-1, keepdims=True)), sink)
            p_c = jnp.exp(s_c - m)
            p_p = jnp.exp(s_p - m)
            denom = (jnp.sum(p_c, axis=-1, keepdims=True) + jnp.sum(p_p, axis=-1, keepdims=True)
                     + jnp.exp(sink - m))
            og = (_bdot(p_c, vc[:, lo:hi]) + _bdot(p_p, vp[:, lo:hi])) / denom
            outs += [og[j * blk:(j + 1) * blk] for j in range(ATT_GROUP)]
        o_ref[0, cur] = jnp.concatenate(outs, axis=1).astype(o_ref.dtype)


def _rope_tables(t):
    half = ROPE_DIM // 2
    inv_freq = ROPE_THETA ** (-(jnp.arange(half, dtype=F32) * 2.0 / ROPE_DIM))
    ang = jnp.arange(t).astype(F32)[:, None] * inv_freq[None, :]
    cos, sin = jnp.cos(ang), jnp.sin(ang)
    rest = ATT_HEAD_DIM - ROPE_DIM
    c = jnp.concatenate([cos, cos, jnp.ones((t, rest), F32)], axis=1)
    sm = jnp.concatenate([-sin, jnp.zeros((t, half + rest), F32)], axis=1)
    sp = jnp.concatenate([jnp.zeros((t, half), F32), sin, jnp.zeros((t, rest), F32)], axis=1)
    return tuple(jnp.tile(a, (1, ATT_HEADS)) for a in (c, sm, sp))


def attention(q, k, v, q_norm, k_norm, sinks):
    b, t, _ = q.shape
    blk = ATT_BLOCK
    c, sm, sp = _rope_tables(t)
    nb = ATT_STEP_BLOCKS
    step = nb * blk
    cur = lambda bi, i: (bi, i, 0)
    prev = lambda bi, i: (bi, jnp.maximum(i * nb - 1, 0), 0)
    tcur = lambda bi, i: (i, 0)
    tprev = lambda bi, i: (jnp.maximum(i * nb - 1, 0), 0)
    const = lambda bi, i: (0, 0)
    return pl.pallas_call(
        _attn_kernel,
        grid=(b, t // step),
        in_specs=[pl.BlockSpec((1, step, ATT_Q_W), cur),
                  pl.BlockSpec((1, step, ATT_KV_W), cur), pl.BlockSpec((1, blk, ATT_KV_W), prev),
                  pl.BlockSpec((1, step, ATT_KV_W), cur), pl.BlockSpec((1, blk, ATT_KV_W), prev),
                  pl.BlockSpec((step, ATT_Q_W), tcur), pl.BlockSpec((step, ATT_Q_W), tcur),
                  pl.BlockSpec((step, ATT_Q_W), tcur),
                  pl.BlockSpec((blk, ATT_KV_W), tprev), pl.BlockSpec((blk, ATT_KV_W), tprev),
                  pl.BlockSpec((blk, ATT_KV_W), tprev),
                  pl.BlockSpec((1, ATT_Q_W), const), pl.BlockSpec((1, ATT_KV_W), const),
                  pl.BlockSpec((ATT_HEADS, LANES), const)],
        out_specs=pl.BlockSpec((1, step, ATT_Q_W), cur),
        out_shape=jax.ShapeDtypeStruct((b, t, ATT_Q_W), BF16),
        compiler_params=_params("parallel", "parallel"),
        name="swa_attention",
    )(q, k, k, v, v, c, sm, sp, c, sm, sp,
      jnp.tile(q_norm, ATT_HEADS).reshape(1, ATT_Q_W), jnp.tile(k_norm, ATT_KV_HEADS).reshape(1, ATT_KV_W),
      jnp.broadcast_to(sinks[:, None], (ATT_HEADS, LANES)))


def _softplus(x):
    return jnp.maximum(x, 0.0) + jnp.log1p(jnp.exp(-jnp.abs(x)))


def _dn_kernel(x_ref, gate_ref, ba_ref, cw_ref, alog_ref, dtb_ref, onw_ref, o_ref, *scratch):
    ck = DN_CHUNK
    nb = x_ref.shape[0]
    halo_refs = scratch[:nb]
    s_refs = scratch[nb:]

    @pl.when(pl.program_id(0) == 0)
    def _():
        for ref in scratch:
            ref[...] = jnp.zeros_like(ref)

    cw = cw_ref[...]
    onw = onw_ref[...]
    neg_a = -jnp.exp(alog_ref[...])
    dtb = dtb_ref[...]
    ri = lax.broadcasted_iota(jnp.int32, (ck, ck), 0)
    ci = lax.broadcasted_iota(jnp.int32, (ck, ck), 1)
    tril = ri >= ci
    strict = ri > ci
    tril_b = jnp.where(tril, 1.0, 0.0).astype(BF16)
    eye_f = jnp.where(ri == ci, 1.0, 0.0).astype(F32)

    chains = [(b, h) for b in range(nb) for h in range(DN_HEADS)]
    ys, betas, gs_all = [], [], []
    for b in range(nb):
        x = x_ref[b]
        xe = jnp.concatenate([halo_refs[b][...], x], axis=0)
        halo_refs[b][...] = x[ck - SUBLANES:ck]
        y = x * cw[DN_CONV - 1:DN_CONV]
        for s in range(1, DN_CONV):
            y = y + xe[SUBLANES - s:SUBLANES - s + ck] * cw[DN_CONV - 1 - s:DN_CONV - s]
        ys.append(y * jax.nn.sigmoid(y))
        ba = ba_ref[b]
        betas.append(jax.nn.sigmoid(ba))
        gs_all.append(neg_a * _softplus(ba + dtb))

    qs, ks, kbs, vbs, gbs = [], [], [], [], []
    for b, h in chains:
        y = ys[b]
        q = y[:, h * DN_DIM:(h + 1) * DN_DIM]
        k = y[:, DN_W + h * DN_DIM:DN_W + (h + 1) * DN_DIM]
        v = y[:, 2 * DN_W + h * DN_DIM:2 * DN_W + (h + 1) * DN_DIM]
        q = q * lax.rsqrt(jnp.sum(q * q, axis=-1, keepdims=True) + EPS) * (DN_DIM ** -0.5)
        k = k * lax.rsqrt(jnp.sum(k * k, axis=-1, keepdims=True) + EPS)
        beta = betas[b][:, h:h + 1]
        qs.append(q)
        ks.append(k)
        kbs.append(k * beta)
        vbs.append(v * beta)
        gbs.append(jnp.broadcast_to(gs_all[b][:, DN_HEADS + h:DN_HEADS + h + 1], (ck, ck)))
    nc = len(chains)
    gc_i = [_dot_exact_lhs(tril_b, gb) for gb in gbs]
    kq = [lax.dot_general(jnp.concatenate([kbs[c], qs[c]], axis=0).astype(BF16), ks[c].astype(BF16), NT_DIMS,
                          preferred_element_type=F32) for c in range(nc)]
    gcols, glasts, intras, t_mats, pws = [], [], [], [], []
    for c in range(nc):
        gc_j = gc_i[c].T
        decay = jnp.where(tril, jnp.exp(jnp.where(tril, gc_i[c] - gc_j, 0.0)), 0.0)
        gcols.append(gc_i[c][:, 0:1])
        glasts.append(gc_j[:, ck - 1:ck])
        a = -jnp.where(strict, kq[c][:ck] * decay, 0.0)
        intras.append(jnp.where(tril, kq[c][ck:] * decay, 0.0))
        t_mats.append(eye_f + a)
        pws.append(a)
    pws = [_dot_hi(p, p) for p in pws]
    for _ in range(4):
        both = [_dot_hi(pws[c], jnp.concatenate([t_mats[c], pws[c]], axis=1)) for c in range(nc)]
        t_mats = [t_mats[c] + both[c][:, :ck] for c in range(nc)]
        pws = [both[c][:, ck:] for c in range(nc)]
    t_mats = [t_mats[c] + _dot_hi(pws[c], t_mats[c]) for c in range(nc)]
    egcs = [jnp.exp(g) for g in gcols]
    uw = [_bdot(t_mats[c], jnp.concatenate([vbs[c], kbs[c] * egcs[c]], axis=1)) for c in range(nc)]
    sts = [s_refs[c][...] for c in range(nc)]
    ws = [_bdot(jnp.concatenate([uw[c][:, DN_DIM:], qs[c] * egcs[c]], axis=0), sts[c]) for c in range(nc)]
    v_new = [uw[c][:, :DN_DIM] - ws[c][:ck] for c in range(nc)]
    o_intra = [_bdot(intras[c], v_new[c]) for c in range(nc)]
    s_add = [lax.dot_general((ks[c] * jnp.exp(glasts[c] - gcols[c])).astype(BF16), v_new[c].astype(BF16), TN_DIMS,
                             preferred_element_type=F32) for c in range(nc)]
    for c, (b, h) in enumerate(chains):
        egl = jnp.exp(jnp.concatenate([glasts[c], glasts[c]], axis=0))
        s_refs[c][...] = sts[c] * egl + s_add[c]
        o = ws[c][ck:] + o_intra[c]
        on = o * lax.rsqrt(jnp.mean(o * o, axis=-1, keepdims=True) + EPS) * onw
        gt = gate_ref[b, :, h * DN_DIM:(h + 1) * DN_DIM]
        o_ref[b, :, h * DN_DIM:(h + 1) * DN_DIM] = (on * (gt * jax.nn.sigmoid(gt))).astype(o_ref.dtype)


def _lane_row(vals, offset):
    return jnp.zeros((1, LANES), F32).at[0, offset:offset + vals.shape[0]].set(vals)


def delta_net(dqkv, gate, ba, conv_w, a_log, dt_bias, out_norm):
    b, t, _ = dqkv.shape
    ck = DN_CHUNK
    blk = lambda c: (0, c, 0)
    const = lambda c: (0, 0)
    return pl.pallas_call(
        _dn_kernel,
        grid=(t // ck,),
        in_specs=[pl.BlockSpec((b, ck, DN_QKV_W), blk), pl.BlockSpec((b, ck, DN_W), blk),
                  pl.BlockSpec((b, ck, LANES), blk),
                  pl.BlockSpec((DN_CONV, DN_QKV_W), const),
                  pl.BlockSpec((1, LANES), const), pl.BlockSpec((1, LANES), const),
                  pl.BlockSpec((1, DN_DIM), const)],
        out_specs=pl.BlockSpec((b, ck, DN_W), blk),
        out_shape=jax.ShapeDtypeStruct((b, t, DN_W), BF16),
        scratch_shapes=([pltpu.VMEM((SUBLANES, DN_QKV_W), F32)] * b
                        + [pltpu.VMEM((DN_DIM, DN_DIM), F32)] * (b * DN_HEADS)),
        compiler_params=_params("arbitrary"),
        name="gated_delta_net",
    )(dqkv, gate, ba, conv_w, _lane_row(a_log, DN_HEADS), _lane_row(dt_bias, DN_HEADS),
      out_norm.reshape(1, DN_DIM))


def _mix_kernel(att_ref, dn_ref, x_ref, wa_ref, wd_ref, n2_ref, wq_ref, x1_ref, h2_ref, qry_ref):
    x1 = (x_ref[...] + jnp.dot(att_ref[...], wa_ref[...], preferred_element_type=F32)
          + jnp.dot(dn_ref[...], wd_ref[...], preferred_element_type=F32))
    x1_ref[...] = x1
    h2 = (x1 * lax.rsqrt(jnp.mean(x1 * x1, axis=-1, keepdims=True) + EPS) * n2_ref[...]).astype(BF16)
    h2_ref[...] = h2
    qry_ref[...] = jnp.dot(h2, wq_ref[...], preferred_element_type=F32)


def mix_proj(att, dn, x2, w_out, norm2_w, w_query, tm=512):
    n, d = x2.shape
    wa = w_out[:ATT_Q_W].astype(BF16)
    wd = w_out[ATT_Q_W:].astype(BF16)
    wq = w_query.astype(BF16)
    qw = wq.shape[1]
    row = lambda i: (i, 0)
    const = lambda i: (0, 0)
    return pl.pallas_call(
        _mix_kernel,
        grid=(n // tm,),
        in_specs=[pl.BlockSpec((tm, ATT_Q_W), row), pl.BlockSpec((tm, DN_W), row), pl.BlockSpec((tm, d), row),
                  pl.BlockSpec(wa.shape, const), pl.BlockSpec(wd.shape, const),
                  pl.BlockSpec((1, d), const), pl.BlockSpec(wq.shape, const)],
        out_specs=[pl.BlockSpec((tm, d), row), pl.BlockSpec((tm, d), row), pl.BlockSpec((tm, qw), row)],
        out_shape=[jax.ShapeDtypeStruct((n, d), F32), jax.ShapeDtypeStruct((n, d), BF16),
                   jax.ShapeDtypeStruct((n, qw), F32)],
        compiler_params=_params("parallel"),
        name="mix_proj",
    )(att, dn, x2, wa, wd, norm2_w.reshape(1, d), wq)


def _merge_exchange_pairs(n):
    pairs = []
    t = max(1, (n - 1).bit_length())
    p = 1 << (t - 1)
    while p > 0:
        q, r, d = 1 << (t - 1), 0, p
        while d > 0:
            pairs += [(i, i + d) for i in range(n - d) if (i & p) == r]
            d, q, r = q - p, q >> 1, p
        p >>= 1
    return pairs


def _sort_desc(xs):
    xs = list(xs)
    for a, b in _merge_exchange_pairs(len(xs)):
        xs[a], xs[b] = jnp.maximum(xs[a], xs[b]), jnp.minimum(xs[a], xs[b])
    return xs


def _bitonic_merge_desc(xs):
    xs = list(xs)
    n = len(xs)
    d = n // 2
    while d > 0:
        for i in range(n):
            if (i & d) == 0:
                xs[i], xs[i + d] = jnp.maximum(xs[i], xs[i + d]), jnp.minimum(xs[i], xs[i + d])
        d //= 2
    return xs


def _top_of_union(xs, shift):
    k = len(xs)
    rolled = [None if x is None else pltpu.roll(x, shift, 0) for x in xs]
    out = []
    for i in range(k):
        a, b = xs[i], rolled[k - 1 - i]
        out.append(b if a is None else a if b is None else jnp.maximum(a, b))
    return out


def _top_sorted_all_sublanes(slabs):
    xs = _sort_desc(slabs)[:PEER_TOPK]
    xs += [None] * (PEER_TOPK - len(xs))
    for shift in (4, 2):
        xs = _bitonic_merge_desc(_top_of_union(xs, shift))
    return _top_of_union(xs, 1)


def _pack_sublanes(vals, sub):
    out = vals[0]
    for s in range(1, SUBLANES):
        out = jnp.where(sub == s, vals[s], out)
    return out


def _route_kernel(q_ref, k1_ref, k2_ref, th_ref, e1_ref, s2_ref, e2_ref):
    q = q_ref[...].astype(BF16)
    s1 = lax.dot_general(k1_ref[...], q[:, :PEER_HALF], NT_DIMS, preferred_element_type=F32)
    s2 = lax.dot_general(k2_ref[...], q[:, PEER_HALF:], NT_DIMS, preferred_element_type=F32)
    nslab = PEER_KEYS // SUBLANES
    s1s = [s1[SUBLANES * v:SUBLANES * (v + 1)] for v in range(nslab)]
    s2s = [s2[SUBLANES * v:SUBLANES * (v + 1)] for v in range(nslab)]
    v1 = _bitonic_merge_desc(_top_sorted_all_sublanes(s1s))
    v2 = _bitonic_merge_desc(_top_sorted_all_sublanes(s2s))
    sub = lax.broadcasted_iota(jnp.int32, v1[0].shape, 0)
    v2a, v2b = _pack_sublanes(v2[:SUBLANES], sub), _pack_sublanes(v2[SUBLANES:], sub)
    v1b = _pack_sublanes(v1[SUBLANES:], sub)
    cands = [v1[0] + v2a, v1[0] + v2b] + [v1[i] + v2a for i in range(1, SUBLANES)] + [v1b + v2[0]]
    top16 = _top_sorted_all_sublanes(cands)
    tau = top16[0]
    for x in top16[1:]:
        tau = jnp.minimum(tau, x)
    top = v1[0] + v2[0]
    z = None
    for c in cands:
        term = jnp.where(c >= tau, jnp.exp(c - top), 0.0)
        z = term if z is None else z + term
    for shift in (4, 2, 1):
        z = z + pltpu.roll(z, shift, 0)
    inv_z = 0.5 / z
    dense_r = 4
    th_rank = []
    for i, jmax in ((0, 16), (1, 8), (2, 5)):
        t = jnp.full_like(tau, jnp.inf)
        for r in range(dense_r, jmax):
            t = jnp.where(v1[i] + v2[r] >= tau, v2[r], t)
        th_rank.append(t)
    for v in range(nslab):
        rows = slice(SUBLANES * v, SUBLANES * (v + 1))
        t = jnp.full_like(tau, jnp.inf)
        for r in range(dense_r):
            t = jnp.where(s1s[v] + v2[r] >= tau, v2[r], t)
        for i in range(len(th_rank)):
            t = jnp.where(s1s[v] == v1[i], jnp.minimum(t, th_rank[i]), t)
        th_ref[0, rows] = t
        e1_ref[0, rows] = jnp.exp(s1s[v] - v1[0]) * inv_z
        s2_ref[0, rows] = s2s[v]
        e2_ref[0, rows] = jnp.exp(s2s[v] - v2[0])


def peer_route(qry, keys1, keys2, tn=1024):
    n = qry.shape[0]
    qblk = 2 * PEER_HALF
    out = pl.BlockSpec((1, PEER_KEYS, tn), lambda i, h: (h, 0, i))
    const = lambda i, h: (0, 0)
    return pl.pallas_call(
        _route_kernel,
        grid=(n // tn, PEER_HEADS),
        in_specs=[pl.BlockSpec((tn, qblk), lambda i, h: (i, h)),
                  pl.BlockSpec((PEER_KEYS, PEER_HALF), const), pl.BlockSpec((PEER_KEYS, PEER_HALF), const)],
        out_specs=[out] * 4,
        out_shape=[jax.ShapeDtypeStruct((PEER_HEADS, PEER_KEYS, n), F32)] * 4,
        compiler_params=_params("parallel", "parallel"),
        name="peer_route",
    )(qry, keys1.astype(BF16), keys2.astype(BF16))


PEER_EXPERT_BLOCK = 2048
PEER_TOKEN_TILE = 512
GATE_ROWS = 32
GATE_KEYS = 4


def _expert_kernel(h_ref, x1_ref, wd_ref, wu_ref, th_ref, e1_ref, s2_ref, e2_ref, o_ref, act_ref, hw_ref, acc_ref):
    e = pl.program_id(1)
    tn = h_ref.shape[0]

    @pl.when(e == 0)
    def _():
        acc_ref[...] = jnp.zeros_like(acc_ref)

    act_ref[...] = lax.dot_general(wd_ref[...], h_ref[...], NT_DIMS, preferred_element_type=F32)
    n_jb = PEER_KEYS // GATE_ROWS
    n_ab = PEER_EXPERT_BLOCK // PEER_KEYS // GATE_KEYS

    def gate_tile(t, carry):
        j0 = pl.multiple_of((t % n_jb) * GATE_ROWS, GATE_ROWS)
        ls = pl.ds(pl.multiple_of((t // n_jb) * LANES, LANES), LANES)
        for ab in range(n_ab):
            keys = range(ab * GATE_KEYS, (ab + 1) * GATE_KEYS)
            gs = [jnp.zeros((GATE_ROWS, LANES), F32) for _ in keys]
            for hd in range(PEER_HEADS):
                s2 = s2_ref[hd, pl.ds(j0, GATE_ROWS), ls]
                e2 = e2_ref[hd, pl.ds(j0, GATE_ROWS), ls]
                for n, al in enumerate(keys):
                    sel = s2 >= th_ref[hd, al:al + 1, ls]
                    gs[n] = gs[n] + jnp.where(sel, e2 * e1_ref[hd, al:al + 1, ls], 0.0)
            for n, al in enumerate(keys):
                r0 = pl.multiple_of(al * PEER_KEYS + j0, GATE_ROWS)
                z = act_ref[pl.ds(r0, GATE_ROWS), ls]
                gelu2 = z * (1.0 + lax.erf(z * (2.0 ** -0.5)))
                hw_ref[pl.ds(r0, GATE_ROWS), ls] = (gs[n] * gelu2).astype(BF16)
        return carry

    lax.fori_loop(0, n_jb * (tn // LANES), gate_tile, 0)
    acc_ref[...] += lax.dot_general(hw_ref[...], wu_ref[...], TN_DIMS, preferred_element_type=F32)

    @pl.when(e == pl.num_programs(1) - 1)
    def _():
        o_ref[...] = x1_ref[...] + acc_ref[...]


def peer_experts(h2, x1, expert_down, expert_up, theta, e1, s2, e2):
    n, d = h2.shape
    ne = expert_down.shape[0]
    tn, eb = PEER_TOKEN_TILE, PEER_EXPERT_BLOCK
    ab = eb // PEER_KEYS
    wd = expert_down.astype(BF16)
    wu = expert_up.astype(BF16)
    tok = lambda i, e: (i, 0)
    full = pl.BlockSpec((PEER_HEADS, PEER_KEYS, tn), lambda i, e: (0, 0, i))
    part = pl.BlockSpec((PEER_HEADS, ab, tn), lambda i, e: (0, e, i))
    return pl.pallas_call(
        _expert_kernel,
        grid=(n // tn, ne // eb),
        in_specs=[pl.BlockSpec((tn, d), tok), pl.BlockSpec((tn, d), tok),
                  pl.BlockSpec((eb, d), lambda i, e: (e, 0)), pl.BlockSpec((eb, d), lambda i, e: (e, 0)),
                  part, part, full, full],
        out_specs=pl.BlockSpec((tn, d), tok),
        out_shape=jax.ShapeDtypeStruct((n, d), F32),
        scratch_shapes=[pltpu.VMEM((eb, tn), F32), pltpu.VMEM((eb, tn), BF16), pltpu.VMEM((tn, d), F32)],
        compiler_params=_params("parallel", "arbitrary"),
        name="peer_experts",
    )(h2, x1, wd, wu, theta, e1, s2, e2)


def kernel(x, norm1_w, w_in, att_q_norm, att_k_norm, att_sinks, dn_conv_w, dn_A_log, dn_dt_bias, dn_out_norm, w_out, norm2_w, peer_w_query, peer_sub_keys1, peer_sub_keys2, peer_expert_down, peer_expert_up):
    b, t, d = x.shape
    x2 = x.reshape(b * t, d)
    for l in range(norm1_w.shape[0]):
        a_q, a_k, a_v, d_qkv, d_gate, d_ba = in_proj(x2, norm1_w[l], w_in[l])
        seq = lambda a: a.reshape(b, t, a.shape[-1])
        att = attention(seq(a_q), seq(a_k), seq(a_v), att_q_norm[l], att_k_norm[l], att_sinks[l])
        dn = delta_net(seq(d_qkv), seq(d_gate), seq(d_ba), dn_conv_w[l], dn_A_log[l], dn_dt_bias[l],
                       dn_out_norm[l])
        x1, h2, qry = mix_proj(att.reshape(b * t, -1), dn.reshape(b * t, -1), x2, w_out[l], norm2_w[l],
                               peer_w_query[l])
        theta, e1, s2, e2 = peer_route(qry, peer_sub_keys1[l], peer_sub_keys2[l])
        x2 = peer_experts(h2, x1, peer_expert_down[l], peer_expert_up[l], theta, e1, s2, e2)
    return x2.reshape(b, t, d)
```

```python
import jax
import jax.numpy as jnp
from jax import lax
from jax.experimental import pallas as pl
from jax.experimental.pallas import tpu as pltpu

EPS = 1e-6
LANES = 128
SUBLANES = 8
VMEM_LIMIT = 56 * 1024 * 1024

ATT_HEADS, ATT_KV_HEADS, ATT_HEAD_DIM = 8, 2, 64
ATT_GROUP = ATT_HEADS // ATT_KV_HEADS
ATT_BLOCK = 128
ATT_STEP_BLOCKS = 8
ROPE_THETA = 500000.0
ROPE_DIM = ATT_HEAD_DIM // 4
DN_HEADS, DN_DIM, DN_CONV, DN_CHUNK = 4, 128, 4, 64
ATT_Q_W = ATT_HEADS * ATT_HEAD_DIM
ATT_KV_W = ATT_KV_HEADS * ATT_HEAD_DIM
DN_W = DN_HEADS * DN_DIM
DN_QKV_W = 3 * DN_W
IN_MAIN_W = ATT_Q_W + 2 * ATT_KV_W + DN_QKV_W + DN_W
PEER_HEADS, PEER_KEYS, PEER_TOPK = 8, 128, 16
PEER_HALF = 128

F32 = jnp.float32
BF16 = jnp.bfloat16
NT_DIMS = (((1,), (1,)), ((), ()))
TN_DIMS = (((0,), (0,)), ((), ()))


def _params(*sem):
    return pltpu.CompilerParams(dimension_semantics=sem, vmem_limit_bytes=VMEM_LIMIT)


def _bdot(a, b):
    return jnp.dot(a.astype(BF16), b.astype(BF16), preferred_element_type=F32)


def _hi_lo(x):
    hi = x.astype(BF16)
    return hi, (x - hi.astype(F32)).astype(BF16)


def _dot_hi(a, b):
    ah, al = _hi_lo(a)
    bh, bl = _hi_lo(b)
    return (jnp.dot(ah, bh, preferred_element_type=F32) + jnp.dot(ah, bl, preferred_element_type=F32)
            + jnp.dot(al, bh, preferred_element_type=F32))


def _dot_exact_lhs(lhs, x):
    hi = x.astype(BF16)
    r = x - hi.astype(F32)
    mid = r.astype(BF16)
    lo = (r - mid.astype(F32)).astype(BF16)
    return (jnp.dot(lhs, hi, preferred_element_type=F32) + jnp.dot(lhs, mid, preferred_element_type=F32)
            + jnp.dot(lhs, lo, preferred_element_type=F32))


def _in_proj_kernel(x_ref, nw_ref, w_ref, q_ref, k_ref, v_ref, dqkv_ref, gate_ref, ba_ref):
    x = x_ref[...]
    h = x * lax.rsqrt(jnp.mean(x * x, axis=-1, keepdims=True) + EPS) * nw_ref[...]
    p = jnp.dot(h.astype(BF16), w_ref[...], preferred_element_type=F32)
    o = 0
    for ref in (q_ref, k_ref, v_ref, dqkv_ref, gate_ref, ba_ref):
        w = ref.shape[-1]
        ref[...] = p[:, o:o + w]
        o += w


def in_proj(x2, norm_w, w_in, tm=512):
    n, d = x2.shape
    small = w_in.shape[1] - IN_MAIN_W
    w_pad = jnp.pad(w_in, ((0, 0), (0, LANES - small))).astype(BF16)
    widths = (ATT_Q_W, ATT_KV_W, ATT_KV_W, DN_QKV_W, DN_W, LANES)
    return pl.pallas_call(
        _in_proj_kernel,
        grid=(n // tm,),
        in_specs=[pl.BlockSpec((tm, d), lambda i: (i, 0)),
                  pl.BlockSpec((1, d), lambda i: (0, 0)),
                  pl.BlockSpec(w_pad.shape, lambda i: (0, 0))],
        out_specs=[pl.BlockSpec((tm, w), lambda i: (i, 0)) for w in widths],
        out_shape=[jax.ShapeDtypeStruct((n, w), F32) for w in widths],
        compiler_params=_params("parallel"),
        name="in_proj",
    )(x2, norm_w.reshape(1, d), w_pad)


def _norm_rope(x, w, c, sm, sp):
    rows, width = x.shape
    x2 = x * x
    parts = []
    for h in range(width // ATT_HEAD_DIM):
        ms = jnp.mean(x2[:, h * ATT_HEAD_DIM:(h + 1) * ATT_HEAD_DIM], axis=-1, keepdims=True)
        parts.append(jnp.broadcast_to(lax.rsqrt(ms + EPS), (rows, ATT_HEAD_DIM)))
    xn = x * jnp.concatenate(parts, axis=1) * w
    half = ROPE_DIM // 2
    return xn * c + pltpu.roll(xn, width - half, 1) * sm + pltpu.roll(xn, half, 1) * sp


def _attn_kernel(q_ref, kc_ref, kp_ref, vc_ref, vp_ref, c_ref, sm_ref, sp_ref,
                 cp_ref, smp_ref, spp_ref, qw_ref, kw_ref, sink_ref, o_ref):
    i = pl.program_id(1)
    blk = ATT_BLOCK
    wide = lambda ref: jnp.concatenate([ref[...]] * (ATT_Q_W // ATT_KV_W), axis=1)
    q_all = _norm_rope(q_ref[0], qw_ref[...], wide(c_ref), wide(sm_ref), wide(sp_ref))
    kw = kw_ref[...]
    kc_all = _norm_rope(kc_ref[0], kw, c_ref[...], sm_ref[...], sp_ref[...])
    kp_first = _norm_rope(kp_ref[0], kw, cp_ref[...], smp_ref[...], spp_ref[...])
    vc_all = vc_ref[0]
    rows = ATT_GROUP * blk
    qi = lax.broadcasted_iota(jnp.int32, (rows, blk), 0) % blk
    kj = lax.broadcasted_iota(jnp.int32, (rows, blk), 1)
    mask_c = kj <= qi
    scale = ATT_HEAD_DIM ** -0.5
    for s in range(ATT_STEP_BLOCKS):
        cur = slice(s * blk, (s + 1) * blk)
        before = slice((s - 1) * blk, s * blk)
        q, kc, vc = q_all[cur], kc_all[cur], vc_all[cur]
        kp = kp_first if s == 0 else kc_all[before]
        vp = vp_ref[0] if s == 0 else vc_all[before]
        mask_p = jnp.logical_and(kj > qi, i > 0) if s == 0 else kj > qi
        outs = []
        for g in range(ATT_KV_HEADS):
            heads = range(g * ATT_GROUP, (g + 1) * ATT_GROUP)
            qg = jnp.concatenate([q[:, h * ATT_HEAD_DIM:(h + 1) * ATT_HEAD_DIM] for h in heads], axis=0)
            sink = jnp.concatenate([jnp.broadcast_to(sink_ref[h:h + 1, 0:1], (blk, 1)) for h in heads], axis=0)
            lo, hi = g * ATT_HEAD_DIM, (g + 1) * ATT_HEAD_DIM
            qb = qg.astype(BF16)
            s_c = lax.dot_general(qb, kc[:, lo:hi].astype(BF16), NT_DIMS, preferred_element_type=F32) * scale
            s_p = lax.dot_general(qb, kp[:, lo:hi].astype(BF16), NT_DIMS, preferred_element_type=F32) * scale
            s_c = jnp.where(mask_c, s_c, -jnp.inf)
            s_p = jnp.where(mask_p, s_p, -jnp.inf)
            m = jnp.maximum(jnp.maximum(jnp.max(s_c, axis=-1, keepdims=True),
                                        jnp.max(s_p, axis=-1, keepdims=True)), sink)
            p_c = jnp.exp(s_c - m)
            p_p = jnp.exp(s_p - m)
            denom = (jnp.sum(p_c, axis=-1, keepdims=True) + jnp.sum(p_p, axis=-1, keepdims=True)
                     + jnp.exp(sink - m))
            og = (_bdot(p_c, vc[:, lo:hi]) + _bdot(p_p, vp[:, lo:hi])) / denom
            outs += [og[j * blk:(j + 1) * blk] for j in range(ATT_GROUP)]
        o_ref[0, cur] = jnp.concatenate(outs, axis=1).astype(o_ref.dtype)


def _rope_tables(t):
    half = ROPE_DIM // 2
    inv_freq = ROPE_THETA ** (-(jnp.arange(half, dtype=F32) * 2.0 / ROPE_DIM))
    ang = jnp.arange(t).astype(F32)[:, None] * inv_freq[None, :]
    cos, sin = jnp.cos(ang), jnp.sin(ang)
    rest = ATT_HEAD_DIM - ROPE_DIM
    c = jnp.concatenate([cos, cos, jnp.ones((t, rest), F32)], axis=1)
    sm = jnp.concatenate([-sin, jnp.zeros((t, half + rest), F32)], axis=1)
    sp = jnp.concatenate([jnp.zeros((t, half), F32), sin, jnp.zeros((t, rest), F32)], axis=1)
    return tuple(jnp.tile(a, (1, ATT_KV_HEADS)) for a in (c, sm, sp))


def attention(q, k, v, q_norm, k_norm, sinks):
    b, t, _ = q.shape
    blk = ATT_BLOCK
    c, sm, sp = _rope_tables(t)
    nb = ATT_STEP_BLOCKS
    step = nb * blk
    cur = lambda bi, i: (bi, i, 0)
    prev = lambda bi, i: (bi, jnp.maximum(i * nb - 1, 0), 0)
    tcur = lambda bi, i: (i, 0)
    tprev = lambda bi, i: (jnp.maximum(i * nb - 1, 0), 0)
    const = lambda bi, i: (0, 0)
    return pl.pallas_call(
        _attn_kernel,
        grid=(b, t // step),
        in_specs=[pl.BlockSpec((1, step, ATT_Q_W), cur),
                  pl.BlockSpec((1, step, ATT_KV_W), cur), pl.BlockSpec((1, blk, ATT_KV_W), prev),
                  pl.BlockSpec((1, step, ATT_KV_W), cur), pl.BlockSpec((1, blk, ATT_KV_W), prev),
                  pl.BlockSpec((step, ATT_KV_W), tcur), pl.BlockSpec((step, ATT_KV_W), tcur),
                  pl.BlockSpec((step, ATT_KV_W), tcur),
                  pl.BlockSpec((blk, ATT_KV_W), tprev), pl.BlockSpec((blk, ATT_KV_W), tprev),
                  pl.BlockSpec((blk, ATT_KV_W), tprev),
                  pl.BlockSpec((1, ATT_Q_W), const), pl.BlockSpec((1, ATT_KV_W), const),
                  pl.BlockSpec((ATT_HEADS, LANES), const)],
        out_specs=pl.BlockSpec((1, step, ATT_Q_W), cur),
        out_shape=jax.ShapeDtypeStruct((b, t, ATT_Q_W), BF16),
        compiler_params=_params("parallel", "parallel"),
        name="swa_attention",
    )(q, k, k, v, v, c, sm, sp, c, sm, sp,
      jnp.tile(q_norm, ATT_HEADS).reshape(1, ATT_Q_W), jnp.tile(k_norm, ATT_KV_HEADS).reshape(1, ATT_KV_W),
      jnp.broadcast_to(sinks[:, None], (ATT_HEADS, LANES)))


def _softplus(x):
    return jnp.maximum(x, 0.0) + jnp.log1p(jnp.exp(-jnp.abs(x)))


def _dn_kernel(x_ref, gate_ref, ba_ref, cw_ref, alog_ref, dtb_ref, onw_ref, o_ref, *scratch):
    ck = DN_CHUNK
    nb = x_ref.shape[0]
    halo_refs = scratch[:nb]
    s_refs = scratch[nb:]

    @pl.when(pl.program_id(0) == 0)
    def _():
        for ref in scratch:
            ref[...] = jnp.zeros_like(ref)

    cw = cw_ref[...]
    onw = onw_ref[...]
    neg_a = -jnp.exp(alog_ref[...])
    dtb = dtb_ref[...]
    ri = lax.broadcasted_iota(jnp.int32, (ck, ck), 0)
    ci = lax.broadcasted_iota(jnp.int32, (ck, ck), 1)
    tril = ri >= ci
    strict = ri > ci
    tril_b = jnp.where(tril, 1.0, 0.0).astype(BF16)
    eye_f = jnp.where(ri == ci, 1.0, 0.0).astype(F32)

    chains = [(b, h) for b in range(nb) for h in range(DN_HEADS)]
    ys, betas, gs_all = [], [], []
    for b in range(nb):
        x = x_ref[b]
        xe = jnp.concatenate([halo_refs[b][...], x], axis=0)
        halo_refs[b][...] = x[ck - SUBLANES:ck]
        y = x * cw[DN_CONV - 1:DN_CONV]
        for s in range(1, DN_CONV):
            y = y + xe[SUBLANES - s:SUBLANES - s + ck] * cw[DN_CONV - 1 - s:DN_CONV - s]
        ys.append(y * jax.nn.sigmoid(y))
        ba = ba_ref[b]
        betas.append(jax.nn.sigmoid(ba))
        gs_all.append(neg_a * _softplus(ba + dtb))

    qs, ks, kbs, vbs, gbs = [], [], [], [], []
    for b, h in chains:
        y = ys[b]
        q = y[:, h * DN_DIM:(h + 1) * DN_DIM]
        k = y[:, DN_W + h * DN_DIM:DN_W + (h + 1) * DN_DIM]
        v = y[:, 2 * DN_W + h * DN_DIM:2 * DN_W + (h + 1) * DN_DIM]
        q = q * lax.rsqrt(jnp.sum(q * q, axis=-1, keepdims=True) + EPS) * (DN_DIM ** -0.5)
        k = k * lax.rsqrt(jnp.sum(k * k, axis=-1, keepdims=True) + EPS)
        beta = betas[b][:, h:h + 1]
        qs.append(q)
        ks.append(k)
        kbs.append(k * beta)
        vbs.append(v * beta)
        gbs.append(jnp.broadcast_to(gs_all[b][:, DN_HEADS + h:DN_HEADS + h + 1], (ck, ck)))
    nc = len(chains)
    gc_i = [_dot_exact_lhs(tril_b, gb) for gb in gbs]
    kq = [lax.dot_general(jnp.concatenate([kbs[c], qs[c]], axis=0).astype(BF16), ks[c].astype(BF16), NT_DIMS,
                          preferred_element_type=F32) for c in range(nc)]
    gcols, glasts, intras, t_mats, pws = [], [], [], [], []
    for c in range(nc):
        gc_j = gc_i[c].T
        decay = jnp.where(tril, jnp.exp(jnp.where(tril, gc_i[c] - gc_j, 0.0)), 0.0)
        gcols.append(gc_i[c][:, 0:1])
        glasts.append(gc_j[:, ck - 1:ck])
        a = -jnp.where(strict, kq[c][:ck] * decay, 0.0)
        intras.append(jnp.where(tril, kq[c][ck:] * decay, 0.0))
        t_mats.append(eye_f + a)
        pws.append(a)
    pws = [_dot_hi(p, p) for p in pws]
    for _ in range(4):
        both = [_dot_hi(pws[c], jnp.concatenate([t_mats[c], pws[c]], axis=1)) for c in range(nc)]
        t_mats = [t_mats[c] + both[c][:, :ck] for c in range(nc)]
        pws = [both[c][:, ck:] for c in range(nc)]
    t_mats = [t_mats[c] + _dot_hi(pws[c], t_mats[c]) for c in range(nc)]
    egcs = [jnp.exp(g) for g in gcols]
    uw = [_bdot(t_mats[c], jnp.concatenate([vbs[c], kbs[c] * egcs[c]], axis=1)) for c in range(nc)]
    sts = [s_refs[c][...] for c in range(nc)]
    ws = [_bdot(jnp.concatenate([uw[c][:, DN_DIM:], qs[c] * egcs[c]], axis=0), sts[c]) for c in range(nc)]
    v_new = [uw[c][:, :DN_DIM] - ws[c][:ck] for c in range(nc)]
    o_intra = [_bdot(intras[c], v_new[c]) for c in range(nc)]
    s_add = [lax.dot_general((ks[c] * jnp.exp(glasts[c] - gcols[c])).astype(BF16), v_new[c].astype(BF16), TN_DIMS,
                             preferred_element_type=F32) for c in range(nc)]
    for c, (b, h) in enumerate(chains):
        egl = jnp.exp(jnp.concatenate([glasts[c], glasts[c]], axis=0))
        s_refs[c][...] = sts[c] * egl + s_add[c]
        o = ws[c][ck:] + o_intra[c]
        on = o * lax.rsqrt(jnp.mean(o * o, axis=-1, keepdims=True) + EPS) * onw
        gt = gate_ref[b, :, h * DN_DIM:(h + 1) * DN_DIM]
        o_ref[b, :, h * DN_DIM:(h + 1) * DN_DIM] = (on * (gt * jax.nn.sigmoid(gt))).astype(o_ref.dtype)


def _lane_row(vals, offset):
    return jnp.zeros((1, LANES), F32).at[0, offset:offset + vals.shape[0]].set(vals)


def delta_net(dqkv, gate, ba, conv_w, a_log, dt_bias, out_norm):
    b, t, _ = dqkv.shape
    ck = DN_CHUNK
    blk = lambda c: (0, c, 0)
    const = lambda c: (0, 0)
    return pl.pallas_call(
        _dn_kernel,
        grid=(t // ck,),
        in_specs=[pl.BlockSpec((b, ck, DN_QKV_W), blk), pl.BlockSpec((b, ck, DN_W), blk),
                  pl.BlockSpec((b, ck, LANES), blk),
                  pl.BlockSpec((DN_CONV, DN_QKV_W), const),
                  pl.BlockSpec((1, LANES), const), pl.BlockSpec((1, LANES), const),
                  pl.BlockSpec((1, DN_DIM), const)],
        out_specs=pl.BlockSpec((b, ck, DN_W), blk),
        out_shape=jax.ShapeDtypeStruct((b, t, DN_W), BF16),
        scratch_shapes=([pltpu.VMEM((SUBLANES, DN_QKV_W), F32)] * b
                        + [pltpu.VMEM((DN_DIM, DN_DIM), F32)] * (b * DN_HEADS)),
        compiler_params=_params("arbitrary"),
        name="gated_delta_net",
    )(dqkv, gate, ba, conv_w, _lane_row(a_log, DN_HEADS), _lane_row(dt_bias, DN_HEADS),
      out_norm.reshape(1, DN_DIM))


def _mix_kernel(att_ref, dn_ref, x_ref, wa_ref, wd_ref, n2_ref, wq_ref, x1_ref, h2_ref, qry_ref):
    x1 = (x_ref[...] + jnp.dot(att_ref[...], wa_ref[...], preferred_element_type=F32)
          + jnp.dot(dn_ref[...], wd_ref[...], preferred_element_type=F32))
    x1_ref[...] = x1
    h2 = (x1 * lax.rsqrt(jnp.mean(x1 * x1, axis=-1, keepdims=True) + EPS) * n2_ref[...]).astype(BF16)
    h2_ref[...] = h2
    qry_ref[...] = jnp.dot(h2, wq_ref[...], preferred_element_type=F32)


def mix_proj(att, dn, x2, w_out, norm2_w, w_query, tm=512):
    n, d = x2.shape
    wa = w_out[:ATT_Q_W].astype(BF16)
    wd = w_out[ATT_Q_W:].astype(BF16)
    wq = w_query.astype(BF16)
    qw = wq.shape[1]
    row = lambda i: (i, 0)
    const = lambda i: (0, 0)
    return pl.pallas_call(
        _mix_kernel,
        grid=(n // tm,),
        in_specs=[pl.BlockSpec((tm, ATT_Q_W), row), pl.BlockSpec((tm, DN_W), row), pl.BlockSpec((tm, d), row),
                  pl.BlockSpec(wa.shape, const), pl.BlockSpec(wd.shape, const),
                  pl.BlockSpec((1, d), const), pl.BlockSpec(wq.shape, const)],
        out_specs=[pl.BlockSpec((tm, d), row), pl.BlockSpec((tm, d), row), pl.BlockSpec((tm, qw), row)],
        out_shape=[jax.ShapeDtypeStruct((n, d), F32), jax.ShapeDtypeStruct((n, d), BF16),
                   jax.ShapeDtypeStruct((n, qw), F32)],
        compiler_params=_params("parallel"),
        name="mix_proj",
    )(att, dn, x2, wa, wd, norm2_w.reshape(1, d), wq)


def _merge_exchange_pairs(n):
    pairs = []
    t = max(1, (n - 1).bit_length())
    p = 1 << (t - 1)
    while p > 0:
        q, r, d = 1 << (t - 1), 0, p
        while d > 0:
            pairs += [(i, i + d) for i in range(n - d) if (i & p) == r]
            d, q, r = q - p, q >> 1, p
        p >>= 1
    return pairs


def _sort_desc(xs):
    xs = list(xs)
    for a, b in _merge_exchange_pairs(len(xs)):
        xs[a], xs[b] = jnp.maximum(xs[a], xs[b]), jnp.minimum(xs[a], xs[b])
    return xs


def _bitonic_merge_desc(xs):
    xs = list(xs)
    n = len(xs)
    d = n // 2
    while d > 0:
        for i in range(n):
            if (i & d) == 0:
                xs[i], xs[i + d] = jnp.maximum(xs[i], xs[i + d]), jnp.minimum(xs[i], xs[i + d])
        d //= 2
    return xs


def _top_of_union(xs, shift):
    k = len(xs)
    rolled = [None if x is None else pltpu.roll(x, shift, 0) for x in xs]
    out = []
    for i in range(k):
        a, b = xs[i], rolled[k - 1 - i]
        out.append(b if a is None else a if b is None else jnp.maximum(a, b))
    return out


def _top_sorted_all_sublanes(slabs):
    xs = _sort_desc(slabs)[:PEER_TOPK]
    xs += [None] * (PEER_TOPK - len(xs))
    for shift in (4, 2):
        xs = _bitonic_merge_desc(_top_of_union(xs, shift))
    return _top_of_union(xs, 1)


def _pack_sublanes(vals, sub):
    out = vals[0]
    for s in range(1, SUBLANES):
        out = jnp.where(sub == s, vals[s], out)
    return out


def _route_kernel(q_ref, k1_ref, k2_ref, th_ref, e1_ref, s2_ref, e2_ref):
    q = q_ref[...].astype(BF16)
    s1 = lax.dot_general(k1_ref[...], q[:, :PEER_HALF], NT_DIMS, preferred_element_type=F32)
    s2 = lax.dot_general(k2_ref[...], q[:, PEER_HALF:], NT_DIMS, preferred_element_type=F32)
    nslab = PEER_KEYS // SUBLANES
    s1s = [s1[SUBLANES * v:SUBLANES * (v + 1)] for v in range(nslab)]
    s2s = [s2[SUBLANES * v:SUBLANES * (v + 1)] for v in range(nslab)]
    v1 = _bitonic_merge_desc(_top_sorted_all_sublanes(s1s))
    v2 = _bitonic_merge_desc(_top_sorted_all_sublanes(s2s))
    sub = lax.broadcasted_iota(jnp.int32, v1[0].shape, 0)
    v2a, v2b = _pack_sublanes(v2[:SUBLANES], sub), _pack_sublanes(v2[SUBLANES:], sub)
    v1b = _pack_sublanes(v1[SUBLANES:], sub)
    cands = [v1[0] + v2a, v1[0] + v2b] + [v1[i] + v2a for i in range(1, SUBLANES)] + [v1b + v2[0]]
    top16 = _top_sorted_all_sublanes(cands)
    tau = top16[0]
    for x in top16[1:]:
        tau = jnp.minimum(tau, x)
    top = v1[0] + v2[0]
    z = None
    for c in cands:
        term = jnp.where(c >= tau, jnp.exp(c - top), 0.0)
        z = term if z is None else z + term
    for shift in (4, 2, 1):
        z = z + pltpu.roll(z, shift, 0)
    inv_z = 0.5 / z
    dense_r = 4
    th_rank = []
    for i, jmax in ((0, 16), (1, 8), (2, 5)):
        t = jnp.full_like(tau, jnp.inf)
        for r in range(dense_r, jmax):
            t = jnp.where(v1[i] + v2[r] >= tau, v2[r], t)
        th_rank.append(t)
    for v in range(nslab):
        rows = slice(SUBLANES * v, SUBLANES * (v + 1))
        t = jnp.full_like(tau, jnp.inf)
        for r in range(dense_r):
            t = jnp.where(s1s[v] + v2[r] >= tau, v2[r], t)
        for i in range(len(th_rank)):
            t = jnp.where(s1s[v] == v1[i], jnp.minimum(t, th_rank[i]), t)
        th_ref[0, rows] = t
        e1_ref[0, rows] = jnp.exp(s1s[v] - v1[0]) * inv_z
        s2_ref[0, rows] = s2s[v]
        e2_ref[0, rows] = jnp.exp(s2s[v] - v2[0])


def peer_route(qry, keys1, keys2, tn=1024):
    n = qry.shape[0]
    qblk = 2 * PEER_HALF
    out = pl.BlockSpec((1, PEER_KEYS, tn), lambda i, h: (h, 0, i))
    const = lambda i, h: (0, 0)
    return pl.pallas_call(
        _route_kernel,
        grid=(n // tn, PEER_HEADS),
        in_specs=[pl.BlockSpec((tn, qblk), lambda i, h: (i, h)),
                  pl.BlockSpec((PEER_KEYS, PEER_HALF), const), pl.BlockSpec((PEER_KEYS, PEER_HALF), const)],
        out_specs=[out] * 4,
        out_shape=[jax.ShapeDtypeStruct((PEER_HEADS, PEER_KEYS, n), F32)] * 4,
        compiler_params=_params("parallel", "parallel"),
        name="peer_route",
    )(qry, keys1.astype(BF16), keys2.astype(BF16))


PEER_EXPERT_BLOCK = 2048
PEER_TOKEN_TILE = 512
GATE_ROWS = 32
GATE_KEYS = 4


def _expert_kernel(h_ref, x1_ref, wd_ref, wu_ref, th_ref, e1_ref, s2_ref, e2_ref, o_ref, act_ref, hw_ref, acc_ref):
    e = pl.program_id(1)
    tn = h_ref.shape[0]

    @pl.when(e == 0)
    def _():
        acc_ref[...] = jnp.zeros_like(acc_ref)

    act_ref[...] = lax.dot_general(wd_ref[...], h_ref[...], NT_DIMS, preferred_element_type=F32)
    n_jb = PEER_KEYS // GATE_ROWS
    n_ab = PEER_EXPERT_BLOCK // PEER_KEYS // GATE_KEYS

    def gate_tile(t, carry):
        j0 = pl.multiple_of((t % n_jb) * GATE_ROWS, GATE_ROWS)
        ls = pl.ds(pl.multiple_of((t // n_jb) * LANES, LANES), LANES)
        for ab in range(n_ab):
            keys = range(ab * GATE_KEYS, (ab + 1) * GATE_KEYS)
            gs = [jnp.zeros((GATE_ROWS, LANES), F32) for _ in keys]
            for hd in range(PEER_HEADS):
                s2 = s2_ref[hd, pl.ds(j0, GATE_ROWS), ls]
                e2 = e2_ref[hd, pl.ds(j0, GATE_ROWS), ls]
                for n, al in enumerate(keys):
                    sel = s2 >= th_ref[hd, al:al + 1, ls]
                    gs[n] = gs[n] + jnp.where(sel, e2 * e1_ref[hd, al:al + 1, ls], 0.0)
            for n, al in enumerate(keys):
                r0 = pl.multiple_of(al * PEER_KEYS + j0, GATE_ROWS)
                z = act_ref[pl.ds(r0, GATE_ROWS), ls]
                gelu2 = z * (1.0 + lax.erf(z * (2.0 ** -0.5)))
                hw_ref[pl.ds(r0, GATE_ROWS), ls] = (gs[n] * gelu2).astype(BF16)
        return carry

    lax.fori_loop(0, n_jb * (tn // LANES), gate_tile, 0)
    acc_ref[...] += lax.dot_general(hw_ref[...], wu_ref[...], TN_DIMS, preferred_element_type=F32)

    @pl.when(e == pl.num_programs(1) - 1)
    def _():
        o_ref[...] = x1_ref[...] + acc_ref[...]


def peer_experts(h2, x1, expert_down, expert_up, theta, e1, s2, e2):
    n, d = h2.shape
    ne = expert_down.shape[0]
    tn, eb = PEER_TOKEN_TILE, PEER_EXPERT_BLOCK
    ab = eb // PEER_KEYS
    wd = expert_down.astype(BF16)
    wu = expert_up.astype(BF16)
    tok = lambda i, e: (i, 0)
    full = pl.BlockSpec((PEER_HEADS, PEER_KEYS, tn), lambda i, e: (0, 0, i))
    part = pl.BlockSpec((PEER_HEADS, ab, tn), lambda i, e: (0, e, i))
    return pl.pallas_call(
        _expert_kernel,
        grid=(n // tn, ne // eb),
        in_specs=[pl.BlockSpec((tn, d), tok), pl.BlockSpec((tn, d), tok),
                  pl.BlockSpec((eb, d), lambda i, e: (e, 0)), pl.BlockSpec((eb, d), lambda i, e: (e, 0)),
                  part, part, full, full],
        out_specs=pl.BlockSpec((tn, d), tok),
        out_shape=jax.ShapeDtypeStruct((n, d), F32),
        scratch_shapes=[pltpu.VMEM((eb, tn), F32), pltpu.VMEM((eb, tn), BF16), pltpu.VMEM((tn, d), F32)],
        compiler_params=_params("parallel", "arbitrary"),
        name="peer_experts",
    )(h2, x1, wd, wu, theta, e1, s2, e2)


def kernel(x, norm1_w, w_in, att_q_norm, att_k_norm, att_sinks, dn_conv_w, dn_A_log, dn_dt_bias, dn_out_norm, w_out, norm2_w, peer_w_query, peer_sub_keys1, peer_sub_keys2, peer_expert_down, peer_expert_up):
    b, t, d = x.shape
    x2 = x.reshape(b * t, d)
    for l in range(norm1_w.shape[0]):
        a_q, a_k, a_v, d_qkv, d_gate, d_ba = in_proj(x2, norm1_w[l], w_in[l])
        seq = lambda a: a.reshape(b, t, a.shape[-1])
        att = attention(seq(a_q), seq(a_k), seq(a_v), att_q_norm[l], att_k_norm[l], att_sinks[l])
        dn = delta_net(seq(d_qkv), seq(d_gate), seq(d_ba), dn_conv_w[l], dn_A_log[l], dn_dt_bias[l],
                       dn_out_norm[l])
        x1, h2, qry = mix_proj(att.reshape(b * t, -1), dn.reshape(b * t, -1), x2, w_out[l], norm2_w[l],
                               peer_w_query[l])
        theta, e1, s2, e2 = peer_route(qry, peer_sub_keys1[l], peer_sub_keys2[l])
        x2 = peer_experts(h2, x1, peer_expert_down[l], peer_expert_up[l], theta, e1, s2, e2)
    return x2.reshape(b, t, d)
```

```python
import jax
import jax.numpy as jnp
from jax import lax
from jax.experimental import pallas as pl
from jax.experimental.pallas import tpu as pltpu

EPS = 1e-6
LANES = 128
SUBLANES = 8
VMEM_LIMIT = 56 * 1024 * 1024

ATT_HEADS, ATT_KV_HEADS, ATT_HEAD_DIM = 8, 2, 64
ATT_GROUP = ATT_HEADS // ATT_KV_HEADS
ATT_BLOCK = 128
ATT_STEP_BLOCKS = 8
ROPE_THETA = 500000.0
ROPE_DIM = ATT_HEAD_DIM // 4
DN_HEADS, DN_DIM, DN_CONV, DN_CHUNK = 4, 128, 4, 64
ATT_Q_W = ATT_HEADS * ATT_HEAD_DIM
ATT_KV_W = ATT_KV_HEADS * ATT_HEAD_DIM
DN_W = DN_HEADS * DN_DIM
DN_QKV_W = 3 * DN_W
IN_MAIN_W = ATT_Q_W + 2 * ATT_KV_W + DN_QKV_W + DN_W
PEER_HEADS, PEER_KEYS, PEER_TOPK = 8, 128, 16
PEER_HALF = 128

F32 = jnp.float32
BF16 = jnp.bfloat16
NT_DIMS = (((1,), (1,)), ((), ()))
TN_DIMS = (((0,), (0,)), ((), ()))


def _params(*sem):
    return pltpu.CompilerParams(dimension_semantics=sem, vmem_limit_bytes=VMEM_LIMIT)


def _bdot(a, b):
    return jnp.dot(a.astype(BF16), b.astype(BF16), preferred_element_type=F32)


def _hi_lo(x):
    hi = x.astype(BF16)
    return hi, (x - hi.astype(F32)).astype(BF16)


def _dot_hi(a, b):
    ah, al = _hi_lo(a)
    bh, bl = _hi_lo(b)
    return (jnp.dot(ah, bh, preferred_element_type=F32) + jnp.dot(ah, bl, preferred_element_type=F32)
            + jnp.dot(al, bh, preferred_element_type=F32))


def _dot_exact_lhs(lhs, x):
    hi = x.astype(BF16)
    r = x - hi.astype(F32)
    mid = r.astype(BF16)
    lo = (r - mid.astype(F32)).astype(BF16)
    return (jnp.dot(lhs, hi, preferred_element_type=F32) + jnp.dot(lhs, mid, preferred_element_type=F32)
            + jnp.dot(lhs, lo, preferred_element_type=F32))


def _in_proj_kernel(x_ref, nw_ref, w_ref, q_ref, k_ref, v_ref, dqkv_ref, gate_ref, ba_ref):
    x = x_ref[...]
    h = x * lax.rsqrt(jnp.mean(x * x, axis=-1, keepdims=True) + EPS) * nw_ref[...]
    p = jnp.dot(h.astype(BF16), w_ref[...], preferred_element_type=F32)
    o = 0
    for ref in (q_ref, k_ref, v_ref, dqkv_ref, gate_ref, ba_ref):
        w = ref.shape[-1]
        ref[...] = p[:, o:o + w]
        o += w


def in_proj(x2, norm_w, w_in, tm=512):
    n, d = x2.shape
    small = w_in.shape[1] - IN_MAIN_W
    w_pad = jnp.pad(w_in, ((0, 0), (0, LANES - small))).astype(BF16)
    widths = (ATT_Q_W, ATT_KV_W, ATT_KV_W, DN_QKV_W, DN_W, LANES)
    return pl.pallas_call(
        _in_proj_kernel,
        grid=(n // tm,),
        in_specs=[pl.BlockSpec((tm, d), lambda i: (i, 0)),
                  pl.BlockSpec((1, d), lambda i: (0, 0)),
                  pl.BlockSpec(w_pad.shape, lambda i: (0, 0))],
        out_specs=[pl.BlockSpec((tm, w), lambda i: (i, 0)) for w in widths],
        out_shape=[jax.ShapeDtypeStruct((n, w), F32) for w in widths],
        compiler_params=_params("parallel"),
        name="in_proj",
    )(x2, norm_w.reshape(1, d), w_pad)


def _norm_rope(x, w, c, sm, sp):
    rows, width = x.shape
    x2 = x * x
    parts = []
    for h in range(width // ATT_HEAD_DIM):
        ms = jnp.mean(x2[:, h * ATT_HEAD_DIM:(h + 1) * ATT_HEAD_DIM], axis=-1, keepdims=True)
        parts.append(jnp.broadcast_to(lax.rsqrt(ms + EPS), (rows, ATT_HEAD_DIM)))
    xn = x * jnp.concatenate(parts, axis=1) * w
    half = ROPE_DIM // 2
    return xn * c + pltpu.roll(xn, width - half, 1) * sm + pltpu.roll(xn, half, 1) * sp


def _attn_kernel(q_ref, kc_ref, kp_ref, vc_ref, vp_ref, c_ref, sm_ref, sp_ref,
                 cp_ref, smp_ref, spp_ref, qw_ref, kw_ref, sink_ref, o_ref):
    i = pl.program_id(1)
    blk = ATT_BLOCK
    wide = lambda ref: jnp.concatenate([ref[...]] * (ATT_Q_W // ATT_KV_W), axis=1)
    q_all = _norm_rope(q_ref[0], qw_ref[...], wide(c_ref), wide(sm_ref), wide(sp_ref))
    kw = kw_ref[...]
    kc_all = _norm_rope(kc_ref[0], kw, c_ref[...], sm_ref[...], sp_ref[...])
    kp_first = _norm_rope(kp_ref[0], kw, cp_ref[...], smp_ref[...], spp_ref[...])
    vc_all = vc_ref[0]
    rows = ATT_GROUP * blk
    qi = lax.broadcasted_iota(jnp.int32, (rows, blk), 0) % blk
    kj = lax.broadcasted_iota(jnp.int32, (rows, blk), 1)
    mask_c = kj <= qi
    scale = ATT_HEAD_DIM ** -0.5
    for s in range(ATT_STEP_BLOCKS):
        cur = slice(s * blk, (s + 1) * blk)
        before = slice((s - 1) * blk, s * blk)
        q, kc, vc = q_all[cur], kc_all[cur], vc_all[cur]
        kp = kp_first if s == 0 else kc_all[before]
        vp = vp_ref[0] if s == 0 else vc_all[before]
        mask_p = jnp.logical_and(kj > qi, i > 0) if s == 0 else kj > qi
        outs = []
        for g in range(ATT_KV_HEADS):
            heads = range(g * ATT_GROUP, (g + 1) * ATT_GROUP)
            qg = jnp.concatenate([q[:, h * ATT_HEAD_DIM:(h + 1) * ATT_HEAD_DIM] for h in heads], axis=0)
            sink = jnp.concatenate([jnp.broadcast_to(sink_ref[h:h + 1, 0:1], (blk, 1)) for h in heads], axis=0)
            lo, hi = g * ATT_HEAD_DIM, (g + 1) * ATT_HEAD_DIM
            qb = qg.astype(BF16)
            s_c = lax.dot_general(qb, kc[:, lo:hi].astype(BF16), NT_DIMS, preferred_element_type=F32) * scale
            s_p = lax.dot_general(qb, kp[:, lo:hi].astype(BF16), NT_DIMS, preferred_element_type=F32) * scale
            s_c = jnp.where(mask_c, s_c, -jnp.inf)
            s_p = jnp.where(mask_p, s_p, -jnp.inf)
            m = jnp.maximum(jnp.maximum(jnp.max(s_c, axis=-1, keepdims=True),
                                        jnp.max(s_p, axis=-1, keepdims=True)), sink)
            p_c = jnp.exp(s_c - m)
            p_p = jnp.exp(s_p - m)
            denom = (jnp.sum(p_c, axis=-1, keepdims=True) + jnp.sum(p_p, axis=-1, keepdims=True)
                     + jnp.exp(sink - m))
            og = (_bdot(p_c, vc[:, lo:hi]) + _bdot(p_p, vp[:, lo:hi])) / denom
            outs += [og[j * blk:(j + 1) * blk] for j in range(ATT_GROUP)]
        o_ref[0, cur] = jnp.concatenate(outs, axis=1).astype(o_ref.dtype)


def _rope_tables(t):
    half = ROPE_DIM // 2
    inv_freq = ROPE_THETA ** (-(jnp.arange(half, dtype=F32) * 2.0 / ROPE_DIM))
    ang = jnp.arange(t).astype(F32)[:, None] * inv_freq[None, :]
    cos, sin = jnp.cos(ang), jnp.sin(ang)
    rest = ATT_HEAD_DIM - ROPE_DIM
    c = jnp.concatenate([cos, cos, jnp.ones((t, rest), F32)], axis=1)
    sm = jnp.concatenate([-sin, jnp.zeros((t, half + rest), F32)], axis=1)
    sp = jnp.concatenate([jnp.zeros((t, half), F32), sin, jnp.zeros((t, rest), F32)], axis=1)
    return tuple(jnp.tile(a, (1, ATT_KV_HEADS)) for a in (c, sm, sp))


def attention(q, k, v, q_norm, k_norm, sinks):
    b, t, _ = q.shape
    blk = ATT_BLOCK
    c, sm, sp = _rope_tables(t)
    nb = ATT_STEP_BLOCKS
    step = nb * blk
    cur = lambda bi, i: (bi, i, 0)
    prev = lambda bi, i: (bi, jnp.maximum(i * nb - 1, 0), 0)
    tcur = lambda bi, i: (i, 0)
    tprev = lambda bi, i: (jnp.maximum(i * nb - 1, 0), 0)
    const = lambda bi, i: (0, 0)
    return pl.pallas_call(
        _attn_kernel,
        grid=(b, t // step),
        in_specs=[pl.BlockSpec((1, step, ATT_Q_W), cur),
                  pl.BlockSpec((1, step, ATT_KV_W), cur), pl.BlockSpec((1, blk, ATT_KV_W), prev),
                  pl.BlockSpec((1, step, ATT_KV_W), cur), pl.BlockSpec((1, blk, ATT_KV_W), prev),
                  pl.BlockSpec((step, ATT_KV_W), tcur), pl.BlockSpec((step, ATT_KV_W), tcur),
                  pl.BlockSpec((step, ATT_KV_W), tcur),
                  pl.BlockSpec((blk, ATT_KV_W), tprev), pl.BlockSpec((blk, ATT_KV_W), tprev),
                  pl.BlockSpec((blk, ATT_KV_W), tprev),
                  pl.BlockSpec((1, ATT_Q_W), const), pl.BlockSpec((1, ATT_KV_W), const),
                  pl.BlockSpec((ATT_HEADS, LANES), const)],
        out_specs=pl.BlockSpec((1, step, ATT_Q_W), cur),
        out_shape=jax.ShapeDtypeStruct((b, t, ATT_Q_W), BF16),
        compiler_params=_params("parallel", "parallel"),
        name="swa_attention",
    )(q, k, k, v, v, c, sm, sp, c, sm, sp,
      jnp.tile(q_norm, ATT_HEADS).reshape(1, ATT_Q_W), jnp.tile(k_norm, ATT_KV_HEADS).reshape(1, ATT_KV_W),
      jnp.broadcast_to(sinks[:, None], (ATT_HEADS, LANES)))


def _softplus(x):
    return jnp.maximum(x, 0.0) + jnp.log1p(jnp.exp(-jnp.abs(x)))


def _dn_kernel(x_ref, gate_ref, ba_ref, cw_ref, alog_ref, dtb_ref, onw_ref, o_ref, *scratch):
    ck = DN_CHUNK
    nb = x_ref.shape[0]
    halo_refs = scratch[:nb]
    s_refs = scratch[nb:]

    @pl.when(pl.program_id(0) == 0)
    def _():
        for ref in scratch:
            ref[...] = jnp.zeros_like(ref)

    cw = cw_ref[...]
    onw = onw_ref[...]
    neg_a = -jnp.exp(alog_ref[...])
    dtb = dtb_ref[...]
    ri = lax.broadcasted_iota(jnp.int32, (ck, ck), 0)
    ci = lax.broadcasted_iota(jnp.int32, (ck, ck), 1)
    tril = ri >= ci
    strict = ri > ci
    tril_b = jnp.where(tril, 1.0, 0.0).astype(BF16)
    eye_f = jnp.where(ri == ci, 1.0, 0.0).astype(F32)

    chains = [(b, h) for b in range(nb) for h in range(DN_HEADS)]
    ys, betas, gs_all = [], [], []
    for b in range(nb):
        x = x_ref[b]
        xe = jnp.concatenate([halo_refs[b][...], x], axis=0)
        halo_refs[b][...] = x[ck - SUBLANES:ck]
        y = x * cw[DN_CONV - 1:DN_CONV]
        for s in range(1, DN_CONV):
            y = y + xe[SUBLANES - s:SUBLANES - s + ck] * cw[DN_CONV - 1 - s:DN_CONV - s]
        ys.append(y * jax.nn.sigmoid(y))
        ba = ba_ref[b]
        betas.append(jax.nn.sigmoid(ba))
        gs_all.append(neg_a * _softplus(ba + dtb))

    qs, ks, kbs, vbs, gbs = [], [], [], [], []
    for b, h in chains:
        y = ys[b]
        q = y[:, h * DN_DIM:(h + 1) * DN_DIM]
        k = y[:, DN_W + h * DN_DIM:DN_W + (h + 1) * DN_DIM]
        v = y[:, 2 * DN_W + h * DN_DIM:2 * DN_W + (h + 1) * DN_DIM]
        q = q * lax.rsqrt(jnp.sum(q * q, axis=-1, keepdims=True) + EPS) * (DN_DIM ** -0.5)
        k = k * lax.rsqrt(jnp.sum(k * k, axis=-1, keepdims=True) + EPS)
        beta = betas[b][:, h:h + 1]
        qs.append(q)
        ks.append(k)
        kbs.append(k * beta)
        vbs.append(v * beta)
        gbs.append(jnp.broadcast_to(gs_all[b][:, DN_HEADS + h:DN_HEADS + h + 1], (ck, ck)))
    nc = len(chains)
    gc_i = [_dot_exact_lhs(tril_b, gb) for gb in gbs]
    kq = [lax.dot_general(jnp.concatenate([kbs[c], qs[c]], axis=0).astype(BF16), ks[c].astype(BF16), NT_DIMS,
                          preferred_element_type=F32) for c in range(nc)]
    gcols, glasts, intras, t_mats, pws = [], [], [], [], []
    for c in range(nc):
        gc_j = gc_i[c].T
        decay = jnp.where(tril, jnp.exp(jnp.where(tril, gc_i[c] - gc_j, 0.0)), 0.0)
        gcols.append(gc_i[c][:, 0:1])
        glasts.append(gc_j[:, ck - 1:ck])
        a = -jnp.where(strict, kq[c][:ck] * decay, 0.0)
        intras.append(jnp.where(tril, kq[c][ck:] * decay, 0.0))
        t_mats.append(eye_f + a)
        pws.append(a)
    pws = [_dot_hi(p, p) for p in pws]
    for _ in range(4):
        both = [_dot_hi(pws[c], jnp.concatenate([t_mats[c], pws[c]], axis=1)) for c in range(nc)]
        t_mats = [t_mats[c] + both[c][:, :ck] for c in range(nc)]
        pws = [both[c][:, ck:] for c in range(nc)]
    t_mats = [t_mats[c] + _dot_hi(pws[c], t_mats[c]) for c in range(nc)]
    egcs = [jnp.exp(g) for g in gcols]
    uw = [_bdot(t_mats[c], jnp.concatenate([vbs[c], kbs[c] * egcs[c]], axis=1)) for c in range(nc)]
    sts = [s_refs[c][...] for c in range(nc)]
    ws = [_bdot(jnp.concatenate([uw[c][:, DN_DIM:], qs[c] * egcs[c]], axis=0), sts[c]) for c in range(nc)]
    v_new = [uw[c][:, :DN_DIM] - ws[c][:ck] for c in range(nc)]
    o_intra = [_bdot(intras[c], v_new[c]) for c in range(nc)]
    s_add = [lax.dot_general((ks[c] * jnp.exp(glasts[c] - gcols[c])).astype(BF16), v_new[c].astype(BF16), TN_DIMS,
                             preferred_element_type=F32) for c in range(nc)]
    for c, (b, h) in enumerate(chains):
        egl = jnp.exp(jnp.concatenate([glasts[c], glasts[c]], axis=0))
        s_refs[c][...] = sts[c] * egl + s_add[c]
        o = ws[c][ck:] + o_intra[c]
        on = o * lax.rsqrt(jnp.mean(o * o, axis=-1, keepdims=True) + EPS) * onw
        gt = gate_ref[b, :, h * DN_DIM:(h + 1) * DN_DIM]
        o_ref[b, :, h * DN_DIM:(h + 1) * DN_DIM] = (on * (gt * jax.nn.sigmoid(gt))).astype(o_ref.dtype)


def _lane_row(vals, offset):
    return jnp.zeros((1, LANES), F32).at[0, offset:offset + vals.shape[0]].set(vals)


def delta_net(dqkv, gate, ba, conv_w, a_log, dt_bias, out_norm):
    b, t, _ = dqkv.shape
    ck = DN_CHUNK
    blk = lambda c: (0, c, 0)
    const = lambda c: (0, 0)
    return pl.pallas_call(
        _dn_kernel,
        grid=(t // ck,),
        in_specs=[pl.BlockSpec((b, ck, DN_QKV_W), blk), pl.BlockSpec((b, ck, DN_W), blk),
                  pl.BlockSpec((b, ck, LANES), blk),
                  pl.BlockSpec((DN_CONV, DN_QKV_W), const),
                  pl.BlockSpec((1, LANES), const), pl.BlockSpec((1, LANES), const),
                  pl.BlockSpec((1, DN_DIM), const)],
        out_specs=pl.BlockSpec((b, ck, DN_W), blk),
        out_shape=jax.ShapeDtypeStruct((b, t, DN_W), BF16),
        scratch_shapes=([pltpu.VMEM((SUBLANES, DN_QKV_W), F32)] * b
                        + [pltpu.VMEM((DN_DIM, DN_DIM), F32)] * (b * DN_HEADS)),
        compiler_params=_params("arbitrary"),
        name="gated_delta_net",
    )(dqkv, gate, ba, conv_w, _lane_row(a_log, DN_HEADS), _lane_row(dt_bias, DN_HEADS),
      out_norm.reshape(1, DN_DIM))


def _mix_kernel(att_ref, dn_ref, x_ref, wa_ref, wd_ref, n2_ref, wq_ref, x1_ref, h2_ref, qry_ref):
    x1 = (x_ref[...] + jnp.dot(att_ref[...], wa_ref[...], preferred_element_type=F32)
          + jnp.dot(dn_ref[...], wd_ref[...], preferred_element_type=F32))
    x1_ref[...] = x1
    h2 = (x1 * lax.rsqrt(jnp.mean(x1 * x1, axis=-1, keepdims=True) + EPS) * n2_ref[...]).astype(BF16)
    h2_ref[...] = h2
    qry_ref[...] = jnp.dot(h2, wq_ref[...], preferred_element_type=F32)


def mix_proj(att, dn, x2, w_out, norm2_w, w_query, tm=512):
    n, d = x2.shape
    wa = w_out[:ATT_Q_W].astype(BF16)
    wd = w_out[ATT_Q_W:].astype(BF16)
    wq = w_query.astype(BF16)
    qw = wq.shape[1]
    row = lambda i: (i, 0)
    const = lambda i: (0, 0)
    return pl.pallas_call(
        _mix_kernel,
        grid=(n // tm,),
        in_specs=[pl.BlockSpec((tm, ATT_Q_W), row), pl.BlockSpec((tm, DN_W), row), pl.BlockSpec((tm, d), row),
                  pl.BlockSpec(wa.shape, const), pl.BlockSpec(wd.shape, const),
                  pl.BlockSpec((1, d), const), pl.BlockSpec(wq.shape, const)],
        out_specs=[pl.BlockSpec((tm, d), row), pl.BlockSpec((tm, d), row), pl.BlockSpec((tm, qw), row)],
        out_shape=[jax.ShapeDtypeStruct((n, d), F32), jax.ShapeDtypeStruct((n, d), BF16),
                   jax.ShapeDtypeStruct((n, qw), F32)],
        compiler_params=_params("parallel"),
        name="mix_proj",
    )(att, dn, x2, wa, wd, norm2_w.reshape(1, d), wq)


def _merge_exchange_pairs(n):
    pairs = []
    t = max(1, (n - 1).bit_length())
    p = 1 << (t - 1)
    while p > 0:
        q, r, d = 1 << (t - 1), 0, p
        while d > 0:
            pairs += [(i, i + d) for i in range(n - d) if (i & p) == r]
            d, q, r = q - p, q >> 1, p
        p >>= 1
    return pairs


def _sort_desc(xs):
    xs = list(xs)
    for a, b in _merge_exchange_pairs(len(xs)):
        xs[a], xs[b] = jnp.maximum(xs[a], xs[b]), jnp.minimum(xs[a], xs[b])
    return xs


def _bitonic_merge_desc(xs):
    xs = list(xs)
    n = len(xs)
    d = n // 2
    while d > 0:
        for i in range(n):
            if (i & d) == 0:
                xs[i], xs[i + d] = jnp.maximum(xs[i], xs[i + d]), jnp.minimum(xs[i], xs[i + d])
        d //= 2
    return xs


def _top_of_union(xs, shift):
    k = len(xs)
    rolled = [None if x is None else pltpu.roll(x, shift, 0) for x in xs]
    out = []
    for i in range(k):
        a, b = xs[i], rolled[k - 1 - i]
        out.append(b if a is None else a if b is None else jnp.maximum(a, b))
    return out


def _top_sorted_all_sublanes(slabs):
    xs = _sort_desc(slabs)[:PEER_TOPK]
    xs += [None] * (PEER_TOPK - len(xs))
    for shift in (4, 2):
        xs = _bitonic_merge_desc(_top_of_union(xs, shift))
    return _top_of_union(xs, 1)


def _pack_sublanes(vals, sub):
    out = vals[0]
    for s in range(1, SUBLANES):
        out = jnp.where(sub == s, vals[s], out)
    return out


def _route_head(q, k1_ref, k2_ref, th_ref, e1_ref, s2_ref, e2_ref, h):
    q = q.astype(BF16)
    s1 = lax.dot_general(k1_ref[...], q[:, :PEER_HALF], NT_DIMS, preferred_element_type=F32)
    s2 = lax.dot_general(k2_ref[...], q[:, PEER_HALF:], NT_DIMS, preferred_element_type=F32)
    nslab = PEER_KEYS // SUBLANES
    s1s = [s1[SUBLANES * v:SUBLANES * (v + 1)] for v in range(nslab)]
    s2s = [s2[SUBLANES * v:SUBLANES * (v + 1)] for v in range(nslab)]
    v1 = _bitonic_merge_desc(_top_sorted_all_sublanes(s1s))
    v2 = _bitonic_merge_desc(_top_sorted_all_sublanes(s2s))
    sub = lax.broadcasted_iota(jnp.int32, v1[0].shape, 0)
    v2a, v2b = _pack_sublanes(v2[:SUBLANES], sub), _pack_sublanes(v2[SUBLANES:], sub)
    v1b = _pack_sublanes(v1[SUBLANES:], sub)
    cands = [v1[0] + v2a, v1[0] + v2b] + [v1[i] + v2a for i in range(1, SUBLANES)] + [v1b + v2[0]]
    top16 = _top_sorted_all_sublanes(cands)
    tau = top16[0]
    for x in top16[1:]:
        tau = jnp.minimum(tau, x)
    top = v1[0] + v2[0]
    z = None
    for c in cands:
        term = jnp.where(c >= tau, jnp.exp(c - top), 0.0)
        z = term if z is None else z + term
    for shift in (4, 2, 1):
        z = z + pltpu.roll(z, shift, 0)
    inv_z = 0.5 / z
    dense_r = 4
    th_rank = []
    for i, jmax in ((0, 16), (1, 8), (2, 5)):
        t = jnp.full_like(tau, jnp.inf)
        for r in range(dense_r, jmax):
            t = jnp.where(v1[i] + v2[r] >= tau, v2[r], t)
        th_rank.append(t)
    for v in range(nslab):
        rows = slice(SUBLANES * v, SUBLANES * (v + 1))
        t = jnp.full_like(tau, jnp.inf)
        for r in range(dense_r):
            t = jnp.where(s1s[v] + v2[r] >= tau, v2[r], t)
        for i in range(len(th_rank)):
            t = jnp.where(s1s[v] == v1[i], jnp.minimum(t, th_rank[i]), t)
        th_ref[h, rows] = t
        e1_ref[h, rows] = jnp.exp(s1s[v] - v1[0]) * inv_z
        s2_ref[h, rows] = s2s[v]
        e2_ref[h, rows] = jnp.exp(s2s[v] - v2[0])


PEER_EXPERT_BLOCK = 2048
PEER_TOKEN_TILE = 512
GATE_ROWS = 32
GATE_KEYS = 4


def _expert_kernel(h_ref, x1_ref, wd_ref, wu_ref, q_ref, k1_ref, k2_ref, o_ref, act_ref, hw_ref, acc_ref,
                   tha_ref, e1a_ref, s2_ref, e2_ref, th_ref, e1_ref):
    e = pl.program_id(1)
    tn = h_ref.shape[0]
    qw = 2 * PEER_HALF
    nkeys = PEER_EXPERT_BLOCK // PEER_KEYS

    @pl.when(e == 0)
    def _():
        acc_ref[...] = jnp.zeros_like(acc_ref)
        for hd in range(PEER_HEADS):
            _route_head(q_ref[:, hd * qw:(hd + 1) * qw], k1_ref, k2_ref, tha_ref, e1a_ref, s2_ref, e2_ref, hd)

    a_lo = pl.multiple_of(e * nkeys, nkeys)
    th_ref[...] = tha_ref[:, pl.ds(a_lo, nkeys), :]
    e1_ref[...] = e1a_ref[:, pl.ds(a_lo, nkeys), :]

    act_ref[...] = lax.dot_general(wd_ref[...], h_ref[...], NT_DIMS, preferred_element_type=F32)
    n_jb = PEER_KEYS // GATE_ROWS
    n_ab = PEER_EXPERT_BLOCK // PEER_KEYS // GATE_KEYS

    def gate_tile(t, carry):
        j0 = pl.multiple_of((t % n_jb) * GATE_ROWS, GATE_ROWS)
        ls = pl.ds(pl.multiple_of((t // n_jb) * LANES, LANES), LANES)
        for ab in range(n_ab):
            keys = range(ab * GATE_KEYS, (ab + 1) * GATE_KEYS)
            gs = [jnp.zeros((GATE_ROWS, LANES), F32) for _ in keys]
            for hd in range(PEER_HEADS):
                s2 = s2_ref[hd, pl.ds(j0, GATE_ROWS), ls]
                e2 = e2_ref[hd, pl.ds(j0, GATE_ROWS), ls]
                for n, al in enumerate(keys):
                    sel = s2 >= th_ref[hd, al:al + 1, ls]
                    gs[n] = gs[n] + jnp.where(sel, e2 * e1_ref[hd, al:al + 1, ls], 0.0)
            for n, al in enumerate(keys):
                r0 = pl.multiple_of(al * PEER_KEYS + j0, GATE_ROWS)
                z = act_ref[pl.ds(r0, GATE_ROWS), ls]
                gelu2 = z * (1.0 + lax.erf(z * (2.0 ** -0.5)))
                hw_ref[pl.ds(r0, GATE_ROWS), ls] = (gs[n] * gelu2).astype(BF16)
        return carry

    lax.fori_loop(0, n_jb * (tn // LANES), gate_tile, 0)
    acc_ref[...] += lax.dot_general(hw_ref[...], wu_ref[...], TN_DIMS, preferred_element_type=F32)

    @pl.when(e == pl.num_programs(1) - 1)
    def _():
        o_ref[...] = x1_ref[...] + acc_ref[...]


def peer_experts(h2, x1, qry, keys1, keys2, expert_down, expert_up):
    n, d = h2.shape
    ne = expert_down.shape[0]
    tn, eb = PEER_TOKEN_TILE, PEER_EXPERT_BLOCK
    ab = eb // PEER_KEYS
    wd = expert_down.astype(BF16)
    wu = expert_up.astype(BF16)
    tok = lambda i, e: (i, 0)
    const = lambda i, e: (0, 0)
    table = pltpu.VMEM((PEER_HEADS, PEER_KEYS, tn), F32)
    part = pltpu.VMEM((PEER_HEADS, ab, tn), F32)
    return pl.pallas_call(
        _expert_kernel,
        grid=(n // tn, ne // eb),
        in_specs=[pl.BlockSpec((tn, d), tok), pl.BlockSpec((tn, d), tok),
                  pl.BlockSpec((eb, d), lambda i, e: (e, 0)), pl.BlockSpec((eb, d), lambda i, e: (e, 0)),
                  pl.BlockSpec((tn, qry.shape[1]), tok),
                  pl.BlockSpec((PEER_KEYS, PEER_HALF), const), pl.BlockSpec((PEER_KEYS, PEER_HALF), const)],
        out_specs=pl.BlockSpec((tn, d), tok),
        out_shape=jax.ShapeDtypeStruct((n, d), F32),
        scratch_shapes=[pltpu.VMEM((eb, tn), F32), pltpu.VMEM((eb, tn), BF16), pltpu.VMEM((tn, d), F32),
                        table, table, table, table, part, part],
        compiler_params=_params("parallel", "arbitrary"),
        name="peer_experts",
    )(h2, x1, wd, wu, qry, keys1.astype(BF16), keys2.astype(BF16))


def kernel(x, norm1_w, w_in, att_q_norm, att_k_norm, att_sinks, dn_conv_w, dn_A_log, dn_dt_bias, dn_out_norm, w_out, norm2_w, peer_w_query, peer_sub_keys1, peer_sub_keys2, peer_expert_down, peer_expert_up):
    b, t, d = x.shape
    x2 = x.reshape(b * t, d)
    for l in range(norm1_w.shape[0]):
        a_q, a_k, a_v, d_qkv, d_gate, d_ba = in_proj(x2, norm1_w[l], w_in[l])
        seq = lambda a: a.reshape(b, t, a.shape[-1])
        att = attention(seq(a_q), seq(a_k), seq(a_v), att_q_norm[l], att_k_norm[l], att_sinks[l])
        dn = delta_net(seq(d_qkv), seq(d_gate), seq(d_ba), dn_conv_w[l], dn_A_log[l], dn_dt_bias[l],
                       dn_out_norm[l])
        x1, h2, qry = mix_proj(att.reshape(b * t, -1), dn.reshape(b * t, -1), x2, w_out[l], norm2_w[l],
                               peer_w_query[l])
        x2 = peer_experts(h2, x1, qry, peer_sub_keys1[l], peer_sub_keys2[l], peer_expert_down[l],
                          peer_expert_up[l])
    return x2.reshape(b, t, d)
```
